```python
import jax, jax.numpy as jnp
from jax import lax
import numpy as np

D_MODEL = 1024
BATCH = 4
SEQ = 8192
DEPTH = 2

HEAD_DIM = 64
RWKV_WIDTH = D_MODEL // 2
ATTN_WIDTH = D_MODEL - RWKV_WIDTH
RWKV_HEADS = RWKV_WIDTH // HEAD_DIM
ATTN_Q_HEADS = ATTN_WIDTH // HEAD_DIM
ATTN_KV_HEADS = 2
GQA_GROUP = ATTN_Q_HEADS // ATTN_KV_HEADS
KV_WIDTH = ATTN_KV_HEADS * HEAD_DIM
WINDOW = 128
BLOCK = WINDOW
DECAY_LORA = 64
ICLR_LORA = 64
SHIFT_WIDTH = 3 * RWKV_WIDTH + DECAY_LORA + ICLR_LORA
IN_WIDTH = SHIFT_WIDTH + RWKV_WIDTH + ATTN_WIDTH + 2 * KV_WIDTH + ATTN_WIDTH
NORM_EPS = 1e-5
GN_EPS = 64e-5

kernel_name = "hymba_rwkv7_swa_sink_hybrid"


def rmsnorm(x, gain):
    xf = x.astype(jnp.float32)
    y = xf * lax.rsqrt(jnp.mean(xf * xf, axis=-1, keepdims=True) + NORM_EPS)
    return (y * gain.astype(jnp.float32)).astype(x.dtype)


def token_shift(p, mu):
    prev = jnp.pad(p[:, :-1], ((0, 0), (1, 0), (0, 0)))
    return p + (prev - p) * mu


def rwkv7_step(S, inp):
    r_t, w_t, k_t, v_t, a_t, b_t = inp
    sa = jnp.einsum('bhij,bhj->bhi', S, a_t)
    S = S * w_t[:, :, None, :] + sa[..., None] * b_t[:, :, None, :] + v_t[..., None] * k_t[:, :, None, :]
    y = jnp.einsum('bhij,bhj->bhi', S, r_t)
    return S, y


def rwkv7_mix(shifted, g, w0, w_up, a0, a_up, k_k, k_a, r_k, gn_w, gn_b):
    B, T, _ = shifted.shape
    H, N = RWKV_HEADS, HEAD_DIM
    f32 = jnp.float32
    r, k, v, wd, ad = jnp.split(shifted, np.cumsum([RWKV_WIDTH, RWKV_WIDTH, RWKV_WIDTH, DECAY_LORA])[:].tolist(), axis=-1)
    r, k, v, wd, ad = [t.astype(f32) for t in (r, k, v, wd, ad)]
    w = -jax.nn.softplus(-(w0.astype(f32) + jnp.tanh(wd) @ w_up.astype(f32))) - 0.5
    decay = jnp.exp(-jnp.exp(w))
    a = jax.nn.sigmoid(a0.astype(f32) + ad @ a_up.astype(f32))
    kk = (k * k_k.astype(f32)).reshape(B, T, H, N)
    kk = kk / jnp.maximum(jnp.sqrt(jnp.sum(kk * kk, axis=-1, keepdims=True)), 1e-12)
    k = k * (1.0 + (a - 1.0) * k_a.astype(f32))
    rh, wh, kh, vh, ah = [t.reshape(B, T, H, N) for t in (r, decay, k, v, a)]
    a_vec = -kk
    b_vec = kk * ah
    xs = tuple(jnp.moveaxis(t, 1, 0) for t in (rh, wh, kh, vh, a_vec, b_vec))
    S0 = jnp.zeros((B, H, N, N), f32)
    _, y = lax.scan(rwkv7_step, S0, xs)
    y = jnp.moveaxis(y, 0, 1)
    mean = jnp.mean(y, axis=-1, keepdims=True)
    var = jnp.mean((y - mean) ** 2, axis=-1, keepdims=True)
    yn = ((y - mean) * lax.rsqrt(var + GN_EPS)).reshape(B, T, RWKV_WIDTH)
    yn = yn * gn_w.astype(f32) + gn_b.astype(f32)
    bonus = (jnp.sum(rh * kh * r_k.astype(f32), axis=-1, keepdims=True) * vh).reshape(B, T, RWKV_WIDTH)
    out = (yn + bonus) * jax.nn.silu(g.astype(f32))
    return out.astype(shifted.dtype)


def swa_sink_attention(q, k, v, g, sinks):
    B, T, _ = q.shape
    NB = T // BLOCK
    f32 = jnp.float32
    qb = q.reshape(B, NB, BLOCK, ATTN_KV_HEADS, GQA_GROUP, HEAD_DIM)
    kb = k.reshape(B, NB, BLOCK, ATTN_KV_HEADS, HEAD_DIM)
    vb = v.reshape(B, NB, BLOCK, ATTN_KV_HEADS, HEAD_DIM)
    pad = ((0, 0), (1, 0), (0, 0), (0, 0), (0, 0))
    kw = jnp.concatenate([jnp.pad(kb[:, :-1], pad), kb], axis=2)
    vw = jnp.concatenate([jnp.pad(vb[:, :-1], pad), vb], axis=2)
    scale = HEAD_DIM ** -0.5
    s = jnp.einsum('bnqkgd,bnskd->bnkgqs', qb, kw, preferred_element_type=f32) * scale
    qi = jnp.arange(BLOCK)[:, None]
    si = jnp.arange(2 * BLOCK)[None, :]
    band = (si > qi) & (si <= qi + BLOCK)
    valid = (jnp.arange(NB)[:, None, None] > 0) | (si[None] >= BLOCK)
    mask = (band[None] & valid)[None, :, None, None]
    s = jnp.where(mask, s, -jnp.inf)
    sink = sinks.astype(f32).reshape(ATTN_KV_HEADS, GQA_GROUP)[None, None, :, :, None, None]
    m = jnp.maximum(jnp.max(s, axis=-1, keepdims=True), sink)
    p = jnp.exp(s - m)
    denom = jnp.sum(p, axis=-1, keepdims=True) + jnp.exp(sink - m)
    p = p / denom
    o = jnp.einsum('bnkgqs,bnskd->bnqkgd', p, vw.astype(f32)).reshape(B, T, ATTN_WIDTH)
    return (o * jax.nn.silu(g.astype(f32))).astype(q.dtype)


def setup_inputs(seed: int = 0) -> dict:
    key = jax.random.key(seed)
    ks = jax.random.split(key, 18)
    f32 = jnp.float32
    nrm = lambda k, shape, s: jax.random.normal(k, shape, f32) * s
    return {
        "x": jax.random.normal(ks[0], (BATCH, SEQ, D_MODEL), f32),
        "norm_gain": 1.0 + nrm(ks[1], (DEPTH, D_MODEL), 0.02),
        "w_in": nrm(ks[2], (DEPTH, D_MODEL, IN_WIDTH), D_MODEL ** -0.5),
        "shift_mu": jax.random.uniform(ks[3], (DEPTH, SHIFT_WIDTH), f32),
        "w0": jax.random.uniform(ks[4], (DEPTH, RWKV_WIDTH), f32, minval=-6.0, maxval=0.0),
        "w_up": nrm(ks[5], (DEPTH, DECAY_LORA, RWKV_WIDTH), 0.5 * DECAY_LORA ** -0.5),
        "a0": nrm(ks[6], (DEPTH, RWKV_WIDTH), 0.1),
        "a_up": nrm(ks[7], (DEPTH, ICLR_LORA, RWKV_WIDTH), 0.5 * ICLR_LORA ** -0.5),
        "k_k": 0.85 + nrm(ks[8], (DEPTH, RWKV_WIDTH), 0.02),
        "k_a": 1.0 + nrm(ks[9], (DEPTH, RWKV_WIDTH), 0.02),
        "r_k": nrm(ks[10], (DEPTH, RWKV_HEADS, HEAD_DIM), 0.1),
        "gn_w": 1.0 + nrm(ks[11], (DEPTH, RWKV_WIDTH), 0.02),
        "gn_b": nrm(ks[12], (DEPTH, RWKV_WIDTH), 0.02),
        "sinks": nrm(ks[13], (DEPTH, ATTN_Q_HEADS), 1.0),
        "w_out": nrm(ks[14], (DEPTH, D_MODEL, D_MODEL), D_MODEL ** -0.5),
        "final_gain": 1.0 + nrm(ks[15], (D_MODEL,), 0.02),
    }


def reference(x, norm_gain, w_in, shift_mu, w0, w_up, a0, a_up, k_k, k_a, r_k, gn_w, gn_b, sinks, w_out, final_gain):
    splits = np.cumsum([SHIFT_WIDTH, RWKV_WIDTH, ATTN_WIDTH, KV_WIDTH, KV_WIDTH]).tolist()
    for l in range(DEPTH):
        h = rmsnorm(x, norm_gain[l])
        proj = h @ w_in[l]
        p_shift, g_rwkv, q, k_att, v_att, g_attn = jnp.split(proj, splits, axis=-1)
        shifted = token_shift(p_shift, shift_mu[l])
        y_a = rwkv7_mix(shifted, g_rwkv, w0[l], w_up[l], a0[l], a_up[l], k_k[l], k_a[l], r_k[l], gn_w[l], gn_b[l])
        y_b = swa_sink_attention(q, k_att, v_att, g_attn, sinks[l])
        y = jnp.concatenate([y_a, y_b], axis=-1) @ w_out[l]
        x = x + y
    return rmsnorm(x, final_gain)
```

```python
import functools

import numpy as np
import jax
import jax.numpy as jnp
from jax import lax
from jax.experimental import pallas as pl
from jax.experimental.pallas import tpu as pltpu

D_MODEL = 1024
HEAD_DIM = 64
RWKV_WIDTH = 512
ATTN_WIDTH = 512
KV_WIDTH = 128
LORA = 64
SHIFT_WIDTH = 3 * RWKV_WIDTH + 2 * LORA
IN_WIDTH = SHIFT_WIDTH + RWKV_WIDTH + ATTN_WIDTH + 2 * KV_WIDTH + ATTN_WIDTH
NORM_EPS = 1e-5
GN_EPS = 64e-5
WINDOW = 128
CHUNK = 128
LANES = 128
PAIRS = RWKV_WIDTH // LANES
ROW_TILE_IN = 256
ROW_TILE_OUT = 512
VMEM_LIMIT = 48 * 1024 * 1024

F32 = jnp.float32
BF16 = jnp.bfloat16


def _mm(a, b):
    return jnp.dot(a.astype(BF16), b.astype(BF16), preferred_element_type=F32)


def _sigmoid(x):
    return 1.0 / (1.0 + jnp.exp(-x))


def _inproj_kernel(x_ref, gain_ref, w_ref, ps_ref, gr_ref, q_ref, kv_ref, ga_ref):
    x = x_ref[...]
    ms = jnp.mean(x * x, axis=-1, keepdims=True)
    h = (x * lax.rsqrt(ms + NORM_EPS) * gain_ref[...]).astype(BF16)
    c0 = 0
    for ref in (ps_ref, gr_ref, q_ref, kv_ref, ga_ref):
        w = ref.shape[-1]
        ref[...] = jnp.dot(h, w_ref[:, c0:c0 + w], preferred_element_type=F32)
        c0 += w


def _inproj(x2, gain, w_bf16):
    rows = x2.shape[0]
    tm = ROW_TILE_IN
    widths = (SHIFT_WIDTH, RWKV_WIDTH, ATTN_WIDTH, 2 * KV_WIDTH, ATTN_WIDTH)
    return pl.pallas_call(
        _inproj_kernel,
        grid=(rows // tm,),
        in_specs=[
            pl.BlockSpec((tm, D_MODEL), lambda i: (i, 0)),
            pl.BlockSpec((1, D_MODEL), lambda i: (0, 0)),
            pl.BlockSpec((D_MODEL, IN_WIDTH), lambda i: (0, 0)),
        ],
        out_specs=[pl.BlockSpec((tm, w), lambda i: (i, 0)) for w in widths],
        out_shape=[jax.ShapeDtypeStruct((rows, w), F32) for w in widths],
        compiler_params=pltpu.CompilerParams(
            dimension_semantics=("arbitrary",), vmem_limit_bytes=VMEM_LIMIT),
        name="inproj",
    )(x2, gain, w_bf16)


def _unit_lower_inverse(n_mat, row, col):
    nd = jnp.where((row >> 3) == (col >> 3), n_mat, 0.0)
    p = jnp.where(row == col, 1.0, nd)
    ndb = nd.astype(BF16)
    s2 = jnp.dot(ndb, ndb, preferred_element_type=F32)
    ps = _mm(jnp.concatenate([p, s2], axis=0), s2)
    p = p + ps[:CHUNK]
    s4 = ps[CHUNK:]
    d = p + _mm(p, s4)
    for sh in (3, 4, 5, 6):
        e = jnp.where(((row >> (sh + 1)) == (col >> (sh + 1))) & ((row >> sh) != (col >> sh)), n_mat, 0.0)
        d = d + _mm(d, _mm(e, d))
    return d


def _rwkv_kernel(ps_ref, prev_ref, g_ref, mu_ref, w0_ref, wup_ref, a0_ref, aup_ref,
                 kkw_ref, ka_ref, rk_ref, gnw_ref, gnb_ref, ones_ref, tri_ref,
                 out_ref, st_ref):
    c = pl.program_id(1)

    @pl.when(c == 0)
    def _():
        st_ref[...] = jnp.zeros_like(st_ref)

    p = ps_ref[0]
    rows1 = lax.broadcasted_iota(jnp.int32, (CHUNK, 1), 0)
    prev_last = prev_ref[0, 7:8, :] * jnp.where(c > 0, 1.0, 0.0)
    prev = jnp.where(rows1 == 0, prev_last, pltpu.roll(p, 1, 0))
    sh = p + (prev - p) * mu_ref[...]
    r = sh[:, 0:RWKV_WIDTH]
    k = sh[:, RWKV_WIDTH:2 * RWKV_WIDTH]
    v = sh[:, 2 * RWKV_WIDTH:3 * RWKV_WIDTH]
    wa = sh[:, 3 * RWKV_WIDTH:]

    ones = ones_ref[...]
    z = w0_ref[...] + _mm(jnp.tanh(wa), wup_ref[...])
    wlog = -(jnp.maximum(-z, 0.0) + jnp.log(1.0 + jnp.exp(-jnp.abs(z)))) - 0.5
    ld = -jnp.exp(wlog)
    a = _sigmoid(a0_ref[...] + _mm(wa, aup_ref[...]))
    kkraw = k * kkw_ref[...]
    kk = kkraw / jnp.maximum(jnp.sqrt(_mm(kkraw * kkraw, ones)), 1e-12)
    kp = k * (1.0 + (a - 1.0) * ka_ref[...])
    bv = kk * a
    bonus = _mm(r * kp * rk_ref[...], ones) * v

    tri = tri_ref[...]
    l_hi = ld.astype(BF16)
    rem = ld - l_hi.astype(F32)
    l_mid = rem.astype(BF16)
    l_lo = (rem - l_mid.astype(F32)).astype(BF16)
    cs = (jnp.dot(tri, l_hi, preferred_element_type=F32)
          + jnp.dot(tri, l_mid, preferred_element_type=F32)
          + jnp.dot(tri, l_lo, preferred_element_type=F32))
    mid = cs[CHUNK // 2 - 1:CHUNK // 2, :]
    end = cs[CHUNK - 1:CHUNK, :]
    e_out = jnp.exp(mid - cs)
    e_end = jnp.exp(end - cs)
    a_t = -kk * jnp.exp(cs - ld - mid)
    r_t = r * jnp.exp(cs - mid)
    b_t = bv * e_out
    k_t = kp * e_out
    b_h = bv * e_end
    k_h = kp * e_end
    gam = jnp.exp(end)
    g_mid = jnp.exp(mid)
    a_s = a_t * g_mid
    r_s = r_t * g_mid

    b_tT = b_t.T.astype(BF16)
    k_tT = k_t.T.astype(BF16)
    b_hT = b_h.T.astype(BF16)
    k_hT = k_h.T.astype(BF16)

    row = lax.broadcasted_iota(jnp.int32, (CHUNK, CHUNK), 0)
    col = lax.broadcasted_iota(jnp.int32, (CHUNK, CHUNK), 1)
    strict = col < row
    incl = col <= row
    lo = col < HEAD_DIM
    same_head = (row >> 6) == (col >> 6)

    ys = []
    for pr in range(PAIRS):
        sl = slice(pr * LANES, (pr + 1) * LANES)
        at_p = a_t[:, sl]
        rt_p = r_t[:, sl]
        v_sw = pltpu.roll(v[:, sl], HEAD_DIM, 1)
        v_swb = v_sw.astype(BF16)
        rhs_a = jnp.concatenate([b_tT[sl], k_tT[sl]], axis=1)
        zs = []
        res = []
        for hh in range(2):
            own = lo if hh == 0 else jnp.logical_not(lo)
            lhs = jnp.concatenate([jnp.where(own, at_p, 0.0), jnp.where(own, rt_p, 0.0)], axis=0)
            amat = jnp.dot(lhs.astype(BF16), rhs_a, preferred_element_type=F32)
            n_mat = jnp.where(strict, amat[:CHUNK, :CHUNK], 0.0)
            a_ak = jnp.where(strict, amat[:CHUNK, CHUNK:], 0.0)
            a_rb = jnp.where(incl, amat[CHUNK:, :CHUNK], 0.0)
            a_rk = jnp.where(incl, amat[CHUNK:, CHUNK:], 0.0)
            x0 = jnp.where(own, a_s[:, sl], _mm(a_ak, v_swb))
            xs = _mm(_unit_lower_inverse(n_mat, row, col), x0)
            zs.append(_mm(jnp.concatenate([a_rb, a_rk], axis=1),
                          jnp.concatenate([xs, jnp.where(own, 0.0, v_sw)], axis=0)))
            res.append(jnp.dot(b_hT[sl], xs.astype(BF16), preferred_element_type=F32))
        cat = jnp.concatenate([res[0][:HEAD_DIM], res[1][HEAD_DIM:]], axis=0)
        kv = jnp.dot(k_hT[sl], v_swb, preferred_element_type=F32)
        m_t = jnp.where(same_head, cat, 0.0) + jnp.where(row == col, gam[:, sl], 0.0)
        g_t = jnp.where(same_head, 0.0, cat + kv)
        q_hat = r_s[:, sl] + jnp.where(lo, zs[0], zs[1])
        y_loc_sw = jnp.where(lo, zs[1], zs[0])
        st = st_ref[pr]
        y_sw = _mm(q_hat, st) + y_loc_sw
        st_ref[pr] = _mm(m_t, st) + g_t
        ys.append(pltpu.roll(y_sw, HEAD_DIM, 1))
    y = jnp.concatenate(ys, axis=1)

    mean = _mm(y, ones) * (1.0 / HEAD_DIM)
    dlt = y - mean
    var = _mm(dlt * dlt, ones) * (1.0 / HEAD_DIM)
    yn = dlt * lax.rsqrt(var + GN_EPS) * gnw_ref[...] + gnb_ref[...]
    g = g_ref[0]
    out_ref[0] = ((yn + bonus) * (g * _sigmoid(g))).astype(out_ref.dtype)


def _rwkv(ps, g_r, mu, w0, wup_pad, a0, aup_pad, kkw, ka, rk, gnw, gnb, ones, tri):
    bsz, seq, _ = ps.shape
    nc = seq // CHUNK
    sub = CHUNK // 8

    def vec(width):
        return pl.BlockSpec((1, width), lambda b, c: (0, 0))

    return pl.pallas_call(
        _rwkv_kernel,
        grid=(bsz, nc),
        in_specs=[
            pl.BlockSpec((1, CHUNK, SHIFT_WIDTH), lambda b, c: (b, c, 0)),
            pl.BlockSpec((1, 8, SHIFT_WIDTH), lambda b, c: (b, jnp.maximum(c * sub - 1, 0), 0)),
            pl.BlockSpec((1, CHUNK, RWKV_WIDTH), lambda b, c: (b, c, 0)),
            vec(SHIFT_WIDTH),
            vec(RWKV_WIDTH),
            pl.BlockSpec((LANES, RWKV_WIDTH), lambda b, c: (0, 0)),
            vec(RWKV_WIDTH),
            pl.BlockSpec((LANES, RWKV_WIDTH), lambda b, c: (0, 0)),
            vec(RWKV_WIDTH), vec(RWKV_WIDTH), vec(RWKV_WIDTH), vec(RWKV_WIDTH), vec(RWKV_WIDTH),
            pl.BlockSpec((RWKV_WIDTH, RWKV_WIDTH), lambda b, c: (0, 0)),
            pl.BlockSpec((CHUNK, CHUNK), lambda b, c: (0, 0)),
        ],
        out_specs=pl.BlockSpec((1, CHUNK, RWKV_WIDTH), lambda b, c: (b, c, 0)),
        out_shape=jax.ShapeDtypeStruct((bsz, seq, RWKV_WIDTH), BF16),
        scratch_shapes=[pltpu.VMEM((PAIRS, LANES, LANES), F32)],
        compiler_params=pltpu.CompilerParams(
            dimension_semantics=("arbitrary", "arbitrary"), vmem_limit_bytes=VMEM_LIMIT),
        name="rwkv7_chunked",
    )(ps, ps, g_r, mu, w0, wup_pad, a0, aup_pad, kkw, ka, rk, gnw, gnb, ones, tri)


def _attn_kernel(sinks_ref, q_ref, kvc_ref, kvp_ref, g_ref, o_ref):
    n = pl.program_id(1)
    q = q_ref[0]
    kvc = kvc_ref[0]
    kvp = kvp_ref[0]
    kw_t = jnp.concatenate([kvp[:, :KV_WIDTH], kvc[:, :KV_WIDTH]], axis=0).T.astype(BF16)
    vw = jnp.concatenate([kvp[:, KV_WIDTH:], kvc[:, KV_WIDTH:]], axis=0).astype(BF16)
    qi = lax.broadcasted_iota(jnp.int32, (WINDOW, 2 * WINDOW), 0)
    si = lax.broadcasted_iota(jnp.int32, (WINDOW, 2 * WINDOW), 1)
    first_valid = jnp.where(n > 0, 0, WINDOW)
    mask = (si > qi) & (si <= qi + WINDOW) & (si >= first_valid)
    lo = lax.broadcasted_iota(jnp.int32, (WINDOW, LANES), 1) < HEAD_DIM
    scale = HEAD_DIM ** -0.5
    outs = []
    for blk in range(ATTN_WIDTH // LANES):
        qb = q[:, blk * LANES:(blk + 1) * LANES]
        halves = []
        for half in range(2):
            own = lo if half == 0 else jnp.logical_not(lo)
            s = jnp.dot(jnp.where(own, qb, 0.0).astype(BF16), kw_t, preferred_element_type=F32) * scale
            s = jnp.where(mask, s, -1e30)
            sink = sinks_ref[blk + 4 * half]
            m = jnp.maximum(jnp.max(s, axis=-1, keepdims=True), sink)
            p = jnp.exp(s - m)
            denom = jnp.sum(p, axis=-1, keepdims=True) + jnp.exp(sink - m)
            halves.append(jnp.dot(p.astype(BF16), vw, preferred_element_type=F32) / denom)
        outs.append(jnp.where(lo, halves[0], halves[1]))
    o = jnp.concatenate(outs, axis=1)
    g = g_ref[0]
    o_ref[0] = (o * (g * _sigmoid(g))).astype(o_ref.dtype)


def _attn(sinks, q, kv, g_a):
    bsz, seq, _ = q.shape
    nb = seq // WINDOW
    return pl.pallas_call(
        _attn_kernel,
        grid=(bsz, nb),
        in_specs=[
            pl.BlockSpec(memory_space=pltpu.SMEM),
            pl.BlockSpec((1, WINDOW, ATTN_WIDTH), lambda b, n: (b, n, 0)),
            pl.BlockSpec((1, WINDOW, 2 * KV_WIDTH), lambda b, n: (b, n, 0)),
            pl.BlockSpec((1, WINDOW, 2 * KV_WIDTH), lambda b, n: (b, jnp.maximum(n - 1, 0), 0)),
            pl.BlockSpec((1, WINDOW, ATTN_WIDTH), lambda b, n: (b, n, 0)),
        ],
        out_specs=pl.BlockSpec((1, WINDOW, ATTN_WIDTH), lambda b, n: (b, n, 0)),
        out_shape=jax.ShapeDtypeStruct((bsz, seq, ATTN_WIDTH), BF16),
        compiler_params=pltpu.CompilerParams(
            dimension_semantics=("arbitrary", "arbitrary"), vmem_limit_bytes=VMEM_LIMIT),
        name="swa_sink",
    )(sinks, q, kv, kv, g_a)


def _outproj_kernel(ya_ref, yb_ref, x_ref, w_ref, fg_ref, o_ref, *, final):
    y = (jnp.dot(ya_ref[...], w_ref[:RWKV_WIDTH, :], preferred_element_type=F32)
         + jnp.dot(yb_ref[...], w_ref[RWKV_WIDTH:, :], preferred_element_type=F32))
    xn = x_ref[...] + y
    if final:
        ms = jnp.mean(xn * xn, axis=-1, keepdims=True)
        xn = xn * lax.rsqrt(ms + NORM_EPS) * fg_ref[...]
    o_ref[...] = xn


def _outproj(ya, yb, x2, w_bf16, fgain, final):
    rows = x2.shape[0]
    tm = ROW_TILE_OUT
    return pl.pallas_call(
        functools.partial(_outproj_kernel, final=final),
        grid=(rows // tm,),
        in_specs=[
            pl.BlockSpec((tm, RWKV_WIDTH), lambda i: (i, 0)),
            pl.BlockSpec((tm, ATTN_WIDTH), lambda i: (i, 0)),
            pl.BlockSpec((tm, D_MODEL), lambda i: (i, 0)),
            pl.BlockSpec((D_MODEL, D_MODEL), lambda i: (0, 0)),
            pl.BlockSpec((1, D_MODEL), lambda i: (0, 0)),
        ],
        out_specs=pl.BlockSpec((tm, D_MODEL), lambda i: (i, 0)),
        out_shape=jax.ShapeDtypeStruct((rows, D_MODEL), F32),
        compiler_params=pltpu.CompilerParams(
            dimension_semantics=("arbitrary",), vmem_limit_bytes=VMEM_LIMIT),
        name="outproj_final" if final else "outproj",
    )(ya, yb, x2, w_bf16, fgain)


_HEAD_ORDER = np.array([0, 4, 1, 5, 2, 6, 3, 7])
_ATTN_PERM = (_HEAD_ORDER[:, None] * HEAD_DIM + np.arange(HEAD_DIM)[None, :]).reshape(-1)


def kernel(x, norm_gain, w_in, shift_mu, w0, w_up, a0, a_up, k_k, k_a, r_k, gn_w, gn_b, sinks, w_out, final_gain):
    bsz, seq, _ = x.shape
    depth = w_in.shape[0]
    rows = bsz * seq

    q0 = SHIFT_WIDTH + RWKV_WIDTH
    ga0 = q0 + ATTN_WIDTH + 2 * KV_WIDTH
    col_perm = np.arange(IN_WIDTH)
    col_perm[q0:q0 + ATTN_WIDTH] = q0 + _ATTN_PERM
    col_perm[ga0:ga0 + ATTN_WIDTH] = ga0 + _ATTN_PERM
    row_perm = np.arange(D_MODEL)
    row_perm[RWKV_WIDTH:] = RWKV_WIDTH + _ATTN_PERM

    head_id = np.arange(RWKV_WIDTH) // HEAD_DIM
    ones = jnp.asarray(head_id[:, None] == head_id[None, :], dtype=BF16)
    tri = jnp.asarray(np.tril(np.ones((CHUNK, CHUNK), np.float32)), dtype=BF16)
    zpad = jnp.zeros((LORA, RWKV_WIDTH), F32)

    x2 = x.reshape(rows, D_MODEL)
    for l in range(depth):
        w_in_l = w_in[l][:, col_perm].astype(BF16)
        w_out_l = w_out[l][row_perm, :].astype(BF16)
        wup_pad = jnp.concatenate([w_up[l], zpad], axis=0).astype(BF16)
        aup_pad = jnp.concatenate([zpad, a_up[l]], axis=0).astype(BF16)

        ps, g_r, q, kv, g_a = _inproj(x2, norm_gain[l][None, :], w_in_l)
        y_a = _rwkv(ps.reshape(bsz, seq, SHIFT_WIDTH), g_r.reshape(bsz, seq, RWKV_WIDTH),
                    shift_mu[l][None, :], w0[l][None, :], wup_pad, a0[l][None, :], aup_pad,
                    k_k[l][None, :], k_a[l][None, :], r_k[l].reshape(1, RWKV_WIDTH),
                    gn_w[l][None, :], gn_b[l][None, :], ones, tri)
        y_b = _attn(sinks[l], q.reshape(bsz, seq, ATTN_WIDTH), kv.reshape(bsz, seq, 2 * KV_WIDTH),
                    g_a.reshape(bsz, seq, ATTN_WIDTH))
        x2 = _outproj(y_a.reshape(rows, RWKV_WIDTH), y_b.reshape(rows, ATTN_WIDTH), x2, w_out_l,
                      final_gain[None, :], final=(l == depth - 1))
    return x2.reshape(bsz, seq, D_MODEL)
```

```python
import functools

import numpy as np
import jax
import jax.numpy as jnp
from jax import lax
from jax.experimental import pallas as pl
from jax.experimental.pallas import tpu as pltpu

D_MODEL = 1024
HEAD_DIM = 64
RWKV_WIDTH = 512
ATTN_WIDTH = 512
KV_WIDTH = 128
LORA = 64
SHIFT_WIDTH = 3 * RWKV_WIDTH + 2 * LORA
IN_WIDTH = SHIFT_WIDTH + RWKV_WIDTH + ATTN_WIDTH + 2 * KV_WIDTH + ATTN_WIDTH
NORM_EPS = 1e-5
GN_EPS = 64e-5
WINDOW = 128
CHUNK = 128
LANES = 128
PAIRS = RWKV_WIDTH // LANES
ROW_TILE_IN = 256
ROW_TILE_OUT = 512
VMEM_LIMIT = 48 * 1024 * 1024

F32 = jnp.float32
BF16 = jnp.bfloat16


def _mm(a, b):
    return jnp.dot(a.astype(BF16), b.astype(BF16), preferred_element_type=F32)


def _sigmoid(x):
    return 1.0 / (1.0 + jnp.exp(-x))


def _inproj_kernel(x_ref, gain_ref, w_ref, ps_ref, gr_ref, q_ref, kv_ref, ga_ref):
    x = x_ref[...]
    ms = jnp.mean(x * x, axis=-1, keepdims=True)
    h = (x * lax.rsqrt(ms + NORM_EPS) * gain_ref[...]).astype(BF16)
    c0 = 0
    for ref in (ps_ref, gr_ref, q_ref, kv_ref, ga_ref):
        w = ref.shape[-1]
        ref[...] = jnp.dot(h, w_ref[:, c0:c0 + w], preferred_element_type=F32)
        c0 += w


def _inproj(x2, gain, w_bf16):
    rows = x2.shape[0]
    tm = ROW_TILE_IN
    widths = (SHIFT_WIDTH, RWKV_WIDTH, ATTN_WIDTH, 2 * KV_WIDTH, ATTN_WIDTH)
    return pl.pallas_call(
        _inproj_kernel,
        grid=(rows // tm,),
        in_specs=[
            pl.BlockSpec((tm, D_MODEL), lambda i: (i, 0)),
            pl.BlockSpec((1, D_MODEL), lambda i: (0, 0)),
            pl.BlockSpec((D_MODEL, IN_WIDTH), lambda i: (0, 0)),
        ],
        out_specs=[pl.BlockSpec((tm, w), lambda i: (i, 0)) for w in widths],
        out_shape=[jax.ShapeDtypeStruct((rows, w), F32) for w in widths],
        compiler_params=pltpu.CompilerParams(
            dimension_semantics=("arbitrary",), vmem_limit_bytes=VMEM_LIMIT),
        name="inproj",
    )(x2, gain, w_bf16)


def _unit_lower_inverse_all(n_mats, row, col):
    blk8 = (row >> 3) == (col >> 3)
    eye = row == col
    nds = [jnp.where(blk8, n, 0.0) for n in n_mats]
    ps = [jnp.where(eye, 1.0, nd) for nd in nds]
    ndbs = [nd.astype(BF16) for nd in nds]
    s2s = [jnp.dot(ndb, ndb, preferred_element_type=F32) for ndb in ndbs]
    pss = [_mm(jnp.concatenate([p, s2], axis=0), s2) for p, s2 in zip(ps, s2s)]
    ps = [p + x[:CHUNK] for p, x in zip(ps, pss)]
    ds = [p + _mm(p, x[CHUNK:]) for p, x in zip(ps, pss)]
    for sh in (3, 4, 5, 6):
        off = ((row >> (sh + 1)) == (col >> (sh + 1))) & ((row >> sh) != (col >> sh))
        dbs = [d.astype(BF16) for d in ds]
        ts = [jnp.dot(jnp.where(off, n, 0.0).astype(BF16), db, preferred_element_type=F32)
              for n, db in zip(n_mats, dbs)]
        ds = [d + jnp.dot(db, t.astype(BF16), preferred_element_type=F32) for d, db, t in zip(ds, dbs, ts)]
    return ds


def _rwkv_kernel(ps_ref, prev_ref, g_ref, mu_ref, w0_ref, wup_ref, a0_ref, aup_ref,
                 kkw_ref, ka_ref, rk_ref, gnw_ref, gnb_ref, ones_ref, tri_ref,
                 out_ref, st_ref):
    c = pl.program_id(1)

    @pl.when(c == 0)
    def _():
        st_ref[...] = jnp.zeros_like(st_ref)

    p = ps_ref[0]
    rows1 = lax.broadcasted_iota(jnp.int32, (CHUNK, 1), 0)
    prev_last = prev_ref[0, 7:8, :] * jnp.where(c > 0, 1.0, 0.0)
    prev = jnp.where(rows1 == 0, prev_last, pltpu.roll(p, 1, 0))
    sh = p + (prev - p) * mu_ref[...]
    r = sh[:, 0:RWKV_WIDTH]
    k = sh[:, RWKV_WIDTH:2 * RWKV_WIDTH]
    v = sh[:, 2 * RWKV_WIDTH:3 * RWKV_WIDTH]
    wa = sh[:, 3 * RWKV_WIDTH:]

    ones = ones_ref[...]
    z = w0_ref[...] + _mm(jnp.tanh(wa), wup_ref[...])
    wlog = -(jnp.maximum(-z, 0.0) + jnp.log(1.0 + jnp.exp(-jnp.abs(z)))) - 0.5
    ld = -jnp.exp(wlog)
    a = _sigmoid(a0_ref[...] + _mm(wa, aup_ref[...]))
    kkraw = k * kkw_ref[...]
    kk = kkraw / jnp.maximum(jnp.sqrt(_mm(kkraw * kkraw, ones)), 1e-12)
    kp = k * (1.0 + (a - 1.0) * ka_ref[...])
    bv = kk * a
    bonus = _mm(r * kp * rk_ref[...], ones) * v

    tri = tri_ref[...]
    l_hi = ld.astype(BF16)
    rem = ld - l_hi.astype(F32)
    l_mid = rem.astype(BF16)
    l_lo = (rem - l_mid.astype(F32)).astype(BF16)
    cs = (jnp.dot(tri, l_hi, preferred_element_type=F32)
          + jnp.dot(tri, l_mid, preferred_element_type=F32)
          + jnp.dot(tri, l_lo, preferred_element_type=F32))
    mid = cs[CHUNK // 2 - 1:CHUNK // 2, :]
    end = cs[CHUNK - 1:CHUNK, :]
    e_out = jnp.exp(mid - cs)
    e_end = jnp.exp(end - cs)
    a_t = -kk * jnp.exp(cs - ld - mid)
    r_t = r * jnp.exp(cs - mid)
    b_t = bv * e_out
    k_t = kp * e_out
    b_h = bv * e_end
    k_h = kp * e_end
    gam = jnp.exp(end)
    g_mid = jnp.exp(mid)
    a_s = a_t * g_mid
    r_s = r_t * g_mid

    b_tT = b_t.T.astype(BF16)
    k_tT = k_t.T.astype(BF16)
    b_hT = b_h.T.astype(BF16)
    k_hT = k_h.T.astype(BF16)

    row = lax.broadcasted_iota(jnp.int32, (CHUNK, CHUNK), 0)
    col = lax.broadcasted_iota(jnp.int32, (CHUNK, CHUNK), 1)
    strict = col < row
    incl = col <= row
    lo = col < HEAD_DIM
    same_head = (row >> 6) == (col >> 6)

    heads = [(pr, hh) for pr in range(PAIRS) for hh in range(2)]
    sls = [slice(pr * LANES, (pr + 1) * LANES) for pr in range(PAIRS)]
    owns = [lo, jnp.logical_not(lo)]
    v_sws = [pltpu.roll(v[:, sl], HEAD_DIM, 1) for sl in sls]
    v_swbs = [x.astype(BF16) for x in v_sws]
    rhs_as = [jnp.concatenate([b_tT[sl], k_tT[sl]], axis=1) for sl in sls]

    amats = []
    for pr, hh in heads:
        lhs = jnp.concatenate([jnp.where(owns[hh], a_t[:, sls[pr]], 0.0),
                               jnp.where(owns[hh], r_t[:, sls[pr]], 0.0)], axis=0)
        amats.append(jnp.dot(lhs.astype(BF16), rhs_as[pr], preferred_element_type=F32))
    n_mats = [jnp.where(strict, am[:CHUNK, :CHUNK], 0.0) for am in amats]
    x0s = [jnp.where(owns[hh], a_s[:, sls[pr]],
                     _mm(jnp.where(strict, am[:CHUNK, CHUNK:], 0.0), v_swbs[pr]))
           for (pr, hh), am in zip(heads, amats)]
    t_invs = _unit_lower_inverse_all(n_mats, row, col)
    xss = [_mm(t, x0) for t, x0 in zip(t_invs, x0s)]
    zs = []
    res = []
    for (pr, hh), am, xs in zip(heads, amats, xss):
        a_r = jnp.where(jnp.concatenate([incl, incl], axis=1), am[CHUNK:, :], 0.0)
        zs.append(_mm(a_r, jnp.concatenate([xs, jnp.where(owns[hh], 0.0, v_sws[pr])], axis=0)))
        res.append(jnp.dot(b_hT[sls[pr]], xs.astype(BF16), preferred_element_type=F32))
    kvs = [jnp.dot(k_hT[sl], vb, preferred_element_type=F32) for sl, vb in zip(sls, v_swbs)]
    ys = []
    for pr in range(PAIRS):
        sl = sls[pr]
        z0, z1 = zs[2 * pr], zs[2 * pr + 1]
        cat = jnp.concatenate([res[2 * pr][:HEAD_DIM], res[2 * pr + 1][HEAD_DIM:]], axis=0)
        m_t = jnp.where(same_head, cat, 0.0) + jnp.where(row == col, gam[:, sl], 0.0)
        g_t = jnp.where(same_head, 0.0, cat + kvs[pr])
        q_hat = r_s[:, sl] + jnp.where(lo, z0, z1)
        y_loc_sw = jnp.where(lo, z1, z0)
        st = st_ref[pr]
        y_sw = _mm(q_hat, st) + y_loc_sw
        st_ref[pr] = _mm(m_t, st) + g_t
        ys.append(pltpu.roll(y_sw, HEAD_DIM, 1))
    y = jnp.concatenate(ys, axis=1)

    mean = _mm(y, ones) * (1.0 / HEAD_DIM)
    dlt = y - mean
    var = _mm(dlt * dlt, ones) * (1.0 / HEAD_DIM)
    yn = dlt * lax.rsqrt(var + GN_EPS) * gnw_ref[...] + gnb_ref[...]
    g = g_ref[0]
    out_ref[0] = ((yn + bonus) * (g * _sigmoid(g))).astype(out_ref.dtype)


def _rwkv(ps, g_r, mu, w0, wup_pad, a0, aup_pad, kkw, ka, rk, gnw, gnb, ones, tri):
    bsz, seq, _ = ps.shape
    nc = seq // CHUNK
    sub = CHUNK // 8

    def vec(width):
        return pl.BlockSpec((1, width), lambda b, c: (0, 0))

    return pl.pallas_call(
        _rwkv_kernel,
        grid=(bsz, nc),
        in_specs=[
            pl.BlockSpec((1, CHUNK, SHIFT_WIDTH), lambda b, c: (b, c, 0)),
            pl.BlockSpec((1, 8, SHIFT_WIDTH), lambda b, c: (b, jnp.maximum(c * sub - 1, 0), 0)),
            pl.BlockSpec((1, CHUNK, RWKV_WIDTH), lambda b, c: (b, c, 0)),
            vec(SHIFT_WIDTH),
            vec(RWKV_WIDTH),
            pl.BlockSpec((LANES, RWKV_WIDTH), lambda b, c: (0, 0)),
            vec(RWKV_WIDTH),
            pl.BlockSpec((LANES, RWKV_WIDTH), lambda b, c: (0, 0)),
            vec(RWKV_WIDTH), vec(RWKV_WIDTH), vec(RWKV_WIDTH), vec(RWKV_WIDTH), vec(RWKV_WIDTH),
            pl.BlockSpec((RWKV_WIDTH, RWKV_WIDTH), lambda b, c: (0, 0)),
            pl.BlockSpec((CHUNK, CHUNK), lambda b, c: (0, 0)),
        ],
        out_specs=pl.BlockSpec((1, CHUNK, RWKV_WIDTH), lambda b, c: (b, c, 0)),
        out_shape=jax.ShapeDtypeStruct((bsz, seq, RWKV_WIDTH), BF16),
        scratch_shapes=[pltpu.VMEM((PAIRS, LANES, LANES), F32)],
        compiler_params=pltpu.CompilerParams(
            dimension_semantics=("arbitrary", "arbitrary"), vmem_limit_bytes=VMEM_LIMIT),
        name="rwkv7_chunked",
    )(ps, ps, g_r, mu, w0, wup_pad, a0, aup_pad, kkw, ka, rk, gnw, gnb, ones, tri)


def _attn_kernel(sinks_ref, q_ref, kvc_ref, kvp_ref, g_ref, o_ref):
    n = pl.program_id(1)
    q = q_ref[0]
    kvc = kvc_ref[0]
    kvp = kvp_ref[0]
    kw_t = jnp.concatenate([kvp[:, :KV_WIDTH], kvc[:, :KV_WIDTH]], axis=0).T.astype(BF16)
    vw = jnp.concatenate([kvp[:, KV_WIDTH:], kvc[:, KV_WIDTH:]], axis=0).astype(BF16)
    qi = lax.broadcasted_iota(jnp.int32, (WINDOW, 2 * WINDOW), 0)
    si = lax.broadcasted_iota(jnp.int32, (WINDOW, 2 * WINDOW), 1)
    first_valid = jnp.where(n > 0, 0, WINDOW)
    mask = (si > qi) & (si <= qi + WINDOW) & (si >= first_valid)
    lo = lax.broadcasted_iota(jnp.int32, (WINDOW, LANES), 1) < HEAD_DIM
    scale = HEAD_DIM ** -0.5
    outs = []
    for blk in range(ATTN_WIDTH // LANES):
        qb = q[:, blk * LANES:(blk + 1) * LANES]
        halves = []
        for half in range(2):
            own = lo if half == 0 else jnp.logical_not(lo)
            s = jnp.dot(jnp.where(own, qb, 0.0).astype(BF16), kw_t, preferred_element_type=F32) * scale
            s = jnp.where(mask, s, -1e30)
            sink = sinks_ref[blk + 4 * half]
            m = jnp.maximum(jnp.max(s, axis=-1, keepdims=True), sink)
            p = jnp.exp(s - m)
            denom = jnp.sum(p, axis=-1, keepdims=True) + jnp.exp(sink - m)
            halves.append(jnp.dot(p.astype(BF16), vw, preferred_element_type=F32) / denom)
        outs.append(jnp.where(lo, halves[0], halves[1]))
    o = jnp.concatenate(outs, axis=1)
    g = g_ref[0]
    o_ref[0] = (o * (g * _sigmoid(g))).astype(o_ref.dtype)


def _attn(sinks, q, kv, g_a):
    bsz, seq, _ = q.shape
    nb = seq // WINDOW
    return pl.pallas_call(
        _attn_kernel,
        grid=(bsz, nb),
        in_specs=[
            pl.BlockSpec(memory_space=pltpu.SMEM),
            pl.BlockSpec((1, WINDOW, ATTN_WIDTH), lambda b, n: (b, n, 0)),
            pl.BlockSpec((1, WINDOW, 2 * KV_WIDTH), lambda b, n: (b, n, 0)),
            pl.BlockSpec((1, WINDOW, 2 * KV_WIDTH), lambda b, n: (b, jnp.maximum(n - 1, 0), 0)),
            pl.BlockSpec((1, WINDOW, ATTN_WIDTH), lambda b, n: (b, n, 0)),
        ],
        out_specs=pl.BlockSpec((1, WINDOW, ATTN_WIDTH), lambda b, n: (b, n, 0)),
        out_shape=jax.ShapeDtypeStruct((bsz, seq, ATTN_WIDTH), BF16),
        compiler_params=pltpu.CompilerParams(
            dimension_semantics=("arbitrary", "arbitrary"), vmem_limit_bytes=VMEM_LIMIT),
        name="swa_sink",
    )(sinks, q, kv, kv, g_a)


def _outproj_kernel(ya_ref, yb_ref, x_ref, w_ref, fg_ref, o_ref, *, final):
    y = (jnp.dot(ya_ref[...], w_ref[:RWKV_WIDTH, :], preferred_element_type=F32)
         + jnp.dot(yb_ref[...], w_ref[RWKV_WIDTH:, :], preferred_element_type=F32))
    xn = x_ref[...] + y
    if final:
        ms = jnp.mean(xn * xn, axis=-1, keepdims=True)
        xn = xn * lax.rsqrt(ms + NORM_EPS) * fg_ref[...]
    o_ref[...] = xn


def _outproj(ya, yb, x2, w_bf16, fgain, final):
    rows = x2.shape[0]
    tm = ROW_TILE_OUT
    return pl.pallas_call(
        functools.partial(_outproj_kernel, final=final),
        grid=(rows // tm,),
        in_specs=[
            pl.BlockSpec((tm, RWKV_WIDTH), lambda i: (i, 0)),
            pl.BlockSpec((tm, ATTN_WIDTH), lambda i: (i, 0)),
            pl.BlockSpec((tm, D_MODEL), lambda i: (i, 0)),
            pl.BlockSpec((D_MODEL, D_MODEL), lambda i: (0, 0)),
            pl.BlockSpec((1, D_MODEL), lambda i: (0, 0)),
        ],
        out_specs=pl.BlockSpec((tm, D_MODEL), lambda i: (i, 0)),
        out_shape=jax.ShapeDtypeStruct((rows, D_MODEL), F32),
        compiler_params=pltpu.CompilerParams(
            dimension_semantics=("arbitrary",), vmem_limit_bytes=VMEM_LIMIT),
        name="outproj_final" if final else "outproj",
    )(ya, yb, x2, w_bf16, fgain)


_HEAD_ORDER = np.array([0, 4, 1, 5, 2, 6, 3, 7])
_ATTN_PERM = (_HEAD_ORDER[:, None] * HEAD_DIM + np.arange(HEAD_DIM)[None, :]).reshape(-1)


def kernel(x, norm_gain, w_in, shift_mu, w0, w_up, a0, a_up, k_k, k_a, r_k, gn_w, gn_b, sinks, w_out, final_gain):
    bsz, seq, _ = x.shape
    depth = w_in.shape[0]
    rows = bsz * seq

    q0 = SHIFT_WIDTH + RWKV_WIDTH
    ga0 = q0 + ATTN_WIDTH + 2 * KV_WIDTH
    col_perm = np.arange(IN_WIDTH)
    col_perm[q0:q0 + ATTN_WIDTH] = q0 + _ATTN_PERM
    col_perm[ga0:ga0 + ATTN_WIDTH] = ga0 + _ATTN_PERM
    row_perm = np.arange(D_MODEL)
    row_perm[RWKV_WIDTH:] = RWKV_WIDTH + _ATTN_PERM

    head_id = np.arange(RWKV_WIDTH) // HEAD_DIM
    ones = jnp.asarray(head_id[:, None] == head_id[None, :], dtype=BF16)
    tri = jnp.asarray(np.tril(np.ones((CHUNK, CHUNK), np.float32)), dtype=BF16)
    zpad = jnp.zeros((LORA, RWKV_WIDTH), F32)

    x2 = x.reshape(rows, D_MODEL)
    for l in range(depth):
        w_in_l = w_in[l][:, col_perm].astype(BF16)
        w_out_l = w_out[l][row_perm, :].astype(BF16)
        wup_pad = jnp.concatenate([w_up[l], zpad], axis=0).astype(BF16)
        aup_pad = jnp.concatenate([zpad, a_up[l]], axis=0).astype(BF16)

        ps, g_r, q, kv, g_a = _inproj(x2, norm_gain[l][None, :], w_in_l)
        y_a = _rwkv(ps.reshape(bsz, seq, SHIFT_WIDTH), g_r.reshape(bsz, seq, RWKV_WIDTH),
                    shift_mu[l][None, :], w0[l][None, :], wup_pad, a0[l][None, :], aup_pad,
                    k_k[l][None, :], k_a[l][None, :], r_k[l].reshape(1, RWKV_WIDTH),
                    gn_w[l][None, :], gn_b[l][None, :], ones, tri)
        y_b = _attn(sinks[l], q.reshape(bsz, seq, ATTN_WIDTH), kv.reshape(bsz, seq, 2 * KV_WIDTH),
                    g_a.reshape(bsz, seq, ATTN_WIDTH))
        x2 = _outproj(y_a.reshape(rows, RWKV_WIDTH), y_b.reshape(rows, ATTN_WIDTH), x2, w_out_l,
                      final_gain[None, :], final=(l == depth - 1))
    return x2.reshape(bsz, seq, D_MODEL)
```

```python
import functools

import numpy as np
import jax
import jax.numpy as jnp
from jax import lax
from jax.experimental import pallas as pl
from jax.experimental.pallas import tpu as pltpu

D_MODEL = 1024
HEAD_DIM = 64
RWKV_WIDTH = 512
ATTN_WIDTH = 512
KV_WIDTH = 128
LORA = 64
SHIFT_WIDTH = 3 * RWKV_WIDTH + 2 * LORA
IN_WIDTH = SHIFT_WIDTH + RWKV_WIDTH + ATTN_WIDTH + 2 * KV_WIDTH + ATTN_WIDTH
NORM_EPS = 1e-5
GN_EPS = 64e-5
WINDOW = 128
CHUNK = 128
LANES = 128
PAIRS = RWKV_WIDTH // LANES
CHUNKS_PER_STEP = 2
STEP_ROWS = CHUNKS_PER_STEP * CHUNK
ROW_TILE_IN = 256
ROW_TILE_OUT = 512
VMEM_LIMIT = 48 * 1024 * 1024

F32 = jnp.float32
BF16 = jnp.bfloat16


def _mm(a, b):
    return jnp.dot(a.astype(BF16), b.astype(BF16), preferred_element_type=F32)


def _sigmoid(x):
    return 1.0 / (1.0 + jnp.exp(-x))


def _inproj_kernel(x_ref, gain_ref, w_ref, ps_ref, gr_ref, q_ref, kv_ref, ga_ref):
    x = x_ref[...]
    ms = jnp.mean(x * x, axis=-1, keepdims=True)
    h = (x * lax.rsqrt(ms + NORM_EPS) * gain_ref[...]).astype(BF16)
    c0 = 0
    for ref in (ps_ref, gr_ref, q_ref, kv_ref, ga_ref):
        w = ref.shape[-1]
        ref[...] = jnp.dot(h, w_ref[:, c0:c0 + w], preferred_element_type=F32)
        c0 += w


def _inproj(x2, gain, w_bf16):
    rows = x2.shape[0]
    tm = ROW_TILE_IN
    widths = (SHIFT_WIDTH, RWKV_WIDTH, ATTN_WIDTH, 2 * KV_WIDTH, ATTN_WIDTH)
    return pl.pallas_call(
        _inproj_kernel,
        grid=(rows // tm,),
        in_specs=[
            pl.BlockSpec((tm, D_MODEL), lambda i: (i, 0)),
            pl.BlockSpec((1, D_MODEL), lambda i: (0, 0)),
            pl.BlockSpec((D_MODEL, IN_WIDTH), lambda i: (0, 0)),
        ],
        out_specs=[pl.BlockSpec((tm, w), lambda i: (i, 0)) for w in widths],
        out_shape=[jax.ShapeDtypeStruct((rows, w), F32) for w in widths],
        compiler_params=pltpu.CompilerParams(
            dimension_semantics=("arbitrary",), vmem_limit_bytes=VMEM_LIMIT),
        name="inproj",
    )(x2, gain, w_bf16)


def _head_sum(x, ones_pair):
    return jnp.concatenate(
        [_mm(x[:, pr * LANES:(pr + 1) * LANES], ones_pair) for pr in range(PAIRS)], axis=1)


def _swap_pairs(x):
    return jnp.concatenate(
        [pltpu.roll(x[:, pr * LANES:(pr + 1) * LANES], HEAD_DIM, 1) for pr in range(PAIRS)], axis=1)


def _chunk_scratch():
    return [
        pltpu.VMEM((2, 2 * CHUNK, RWKV_WIDTH), BF16),
        pltpu.VMEM((RWKV_WIDTH, 2 * CHUNK), BF16),
        pltpu.VMEM((2 * RWKV_WIDTH, CHUNK), BF16),
        pltpu.VMEM((CHUNK, RWKV_WIDTH), BF16),
        pltpu.VMEM((2, CHUNK, RWKV_WIDTH), BF16),
        pltpu.VMEM((CHUNK, RWKV_WIDTH), F32),
        pltpu.VMEM((CHUNK, RWKV_WIDTH), F32),
        pltpu.VMEM((CHUNK, RWKV_WIDTH), F32),
        pltpu.VMEM((8, RWKV_WIDTH), F32),
    ]


N_CHUNK_SCRATCH = len(_chunk_scratch())


def _frontend(ps_ref, ck, prev_last, params, ones_pair, tri, scr):
    mu_ref, w0_ref, wup_ref, a0_ref, aup_ref, kkw_ref, ka_ref, rk_ref = params
    lhs_ref, rhs_ref, ht_ref, vsw_ref, vso_ref, as_ref, rs_ref, bonus_ref, gam_ref = scr

    p = ps_ref[0, ck * CHUNK:(ck + 1) * CHUNK, :]
    rows1 = lax.broadcasted_iota(jnp.int32, (CHUNK, 1), 0)
    prev =jnp.where(rows1 == 0, prev_last, pltpu.roll(p, 1, 0))
    sh = p + (prev - p) * mu_ref[...]
    r = sh[:, 0:RWKV_WIDTH]
    k = sh[:, RWKV_WIDTH:2 * RWKV_WIDTH]
    v = sh[:, 2 * RWKV_WIDTH:3 * RWKV_WIDTH]
    wa = sh[:, 3 * RWKV_WIDTH:]
    lw = _mm(jnp.tanh(wa), wup_ref[...])
    la = _mm(wa, aup_ref[...])
    kkraw = k * kkw_ref[...]
    n2 = _head_sum(kkraw * kkraw, ones_pair)
    yield

    z = w0_ref[...] + lw
    wlog = -(jnp.maximum(-z, 0.0) + jnp.log(1.0 + jnp.exp(-jnp.abs(z)))) - 0.5
    ld = -jnp.exp(wlog)
    a = _sigmoid(a0_ref[...] + la)
    kk = kkraw / jnp.maximum(jnp.sqrt(n2), 1e-12)
    kp = k * (1.0 + (a - 1.0) * ka_ref[...])
    bv = kk * a
    bonus_ref[...] = _head_sum(r * kp * rk_ref[...], ones_pair) * v
    l_hi = ld.astype(BF16)
    rem = ld - l_hi.astype(F32)
    l_mid = rem.astype(BF16)
    l_lo = (rem - l_mid.astype(F32)).astype(BF16)
    cs = (jnp.dot(tri, l_hi, preferred_element_type=F32)
          + jnp.dot(tri, l_mid, preferred_element_type=F32)
          + jnp.dot(tri, l_lo, preferred_element_type=F32))
    yield

    mid = cs[CHUNK // 2 - 1:CHUNK // 2, :]
    end = cs[CHUNK - 1:CHUNK, :]
    e_out = jnp.exp(mid - cs)
    e_end = jnp.exp(end - cs)
    a_t = -kk * jnp.exp(cs - ld - mid)
    r_t = r * jnp.exp(cs - mid)
    g_mid = jnp.exp(mid)
    as_ref[...] = a_t * g_mid
    rs_ref[...] = r_t * g_mid
    gam_ref[...] = jnp.broadcast_to(jnp.exp(end), gam_ref.shape)
    lo = (lax.broadcasted_iota(jnp.int32, (CHUNK, RWKV_WIDTH), 1) & HEAD_DIM) == 0
    lhs_ref[0, :CHUNK, :] = jnp.where(lo, a_t, 0.0).astype(BF16)
    lhs_ref[0, CHUNK:, :] = jnp.where(lo, r_t, 0.0).astype(BF16)
    lhs_ref[1, :CHUNK, :] = jnp.where(lo, 0.0, a_t).astype(BF16)
    lhs_ref[1, CHUNK:, :] = jnp.where(lo, 0.0, r_t).astype(BF16)
    rhs_ref[:, :CHUNK] = (bv * e_out).T.astype(BF16)
    rhs_ref[:, CHUNK:] = (kp * e_out).T.astype(BF16)
    ht_ref[:RWKV_WIDTH, :] = (bv * e_end).T.astype(BF16)
    ht_ref[RWKV_WIDTH:, :] = (kp * e_end).T.astype(BF16)
    v_sw = _swap_pairs(v)
    vsw_ref[...] = v_sw.astype(BF16)
    vso_ref[0] = jnp.where(lo, 0.0, v_sw).astype(BF16)
    vso_ref[1] = jnp.where(lo, v_sw, 0.0).astype(BF16)
    yield


def _solver(g_ref, gnw_ref, gnb_ref, ones_pair, out_ref, st_ref, scrs):
    row = lax.broadcasted_iota(jnp.int32, (CHUNK, CHUNK), 0)
    col = lax.broadcasted_iota(jnp.int32, (CHUNK, CHUNK), 1)
    strict = col < row
    incl2 = jnp.concatenate([col <= row, col <= row], axis=1)
    lo = col < HEAD_DIM
    owns = [lo, jnp.logical_not(lo)]
    same_head = (row >> 6) == (col >> 6)
    eye = row == col
    blk8 = (row >> 3) == (col >> 3)
    n_ck = len(scrs)
    heads = [(ck, pr, hh) for ck in range(n_ck) for pr in range(PAIRS) for hh in range(2)]
    sls = [slice(pr * LANES, (pr + 1) * LANES) for pr in range(PAIRS)]
    lhs_refs, rhs_refs, ht_refs, vsw_refs, vso_refs, as_refs, rs_refs, bonus_refs, gam_refs = zip(*scrs)

    amats = [jnp.dot(lhs_refs[ck][hh, :, sls[pr]], rhs_refs[ck][sls[pr], :], preferred_element_type=F32)
             for ck, pr, hh in heads]
    yield
    n_mats = [jnp.where(strict, am[:CHUNK, :CHUNK], 0.0) for am in amats]
    avs = [jnp.dot(jnp.where(strict, am[:CHUNK, CHUNK:], 0.0).astype(BF16), vsw_refs[ck][:, sls[pr]],
                   preferred_element_type=F32) for (ck, pr, hh), am in zip(heads, amats)]
    nds = [jnp.where(blk8, n, 0.0) for n in n_mats]
    ndbs = [nd.astype(BF16) for nd in nds]
    s2s = [jnp.dot(ndb, ndb, preferred_element_type=F32) for ndb in ndbs]
    yield
    ps = [jnp.where(eye, 1.0, nd) for nd in nds]
    pss = [_mm(jnp.concatenate([p, s2], axis=0), s2) for p, s2 in zip(ps, s2s)]
    yield
    ps = [p + x[:CHUNK] for p, x in zip(ps, pss)]
    ds = [p + _mm(p, x[CHUNK:]) for p, x in zip(ps, pss)]
    yield
    for sh in (3, 4, 5, 6):
        off = ((row >> (sh + 1)) == (col >> (sh + 1))) & ((row >> sh) != (col >> sh))
        dbs = [d.astype(BF16) for d in ds]
        ts = [jnp.dot(jnp.where(off, n, 0.0).astype(BF16), db, preferred_element_type=F32)
              for n, db in zip(n_mats, dbs)]
        yield
        ds = [d + jnp.dot(db, t.astype(BF16), preferred_element_type=F32) for d, db, t in zip(ds, dbs, ts)]
        yield
    xss = [_mm(d, jnp.where(owns[hh], as_refs[ck][:, sls[pr]], av)).astype(BF16)
           for (ck, pr, hh), d, av in zip(heads, ds, avs)]
    yield
    zs = [_mm(jnp.where(incl2, am[CHUNK:, :], 0.0),
              jnp.concatenate([xs, vso_refs[ck][hh, :, sls[pr]]], axis=0))
          for (ck, pr, hh), am, xs in zip(heads, amats, xss)]
    res = [jnp.dot(ht_refs[ck][sls[pr], :], xs, preferred_element_type=F32)
           for (ck, pr, hh), xs in zip(heads, xss)]
    kvs = [[jnp.dot(ht_refs[ck][RWKV_WIDTH + pr * LANES:RWKV_WIDTH + (pr + 1) * LANES, :],
                    vsw_refs[ck][:, sls[pr]], preferred_element_type=F32) for pr in range(PAIRS)]
           for ck in range(n_ck)]
    yield
    ys = [[] for _ in range(n_ck)]
    for pr in range(PAIRS):
        sl = sls[pr]
        st = st_ref[pr]
        for ck in range(n_ck):
            i0 = (ck * PAIRS + pr) * 2
            z0, z1 = zs[i0], zs[i0 + 1]
            cat = jnp.concatenate([res[i0][:HEAD_DIM], res[i0 + 1][HEAD_DIM:]], axis=0)
            m_t = jnp.where(same_head, cat, 0.0) + jnp.where(eye, gam_refs[ck][0:1, sl], 0.0)
            g_t = jnp.where(same_head, 0.0, cat + kvs[ck][pr])
            q_hat = rs_refs[ck][:, sl] + jnp.where(lo, z0, z1)
            y_loc_sw = jnp.where(lo, z1, z0)
            stb = st.astype(BF16)
            ys[ck].append(_mm(q_hat, stb) + y_loc_sw)
            st = _mm(m_t, stb) + g_t
        st_ref[pr] = st
    yield
    for ck in range(n_ck):
        rows = slice(ck * CHUNK, (ck + 1) * CHUNK)
        y = _swap_pairs(jnp.concatenate(ys[ck], axis=1))
        mean = _head_sum(y, ones_pair) * (1.0 / HEAD_DIM)
        dlt = y - mean
        var = _head_sum(dlt * dlt, ones_pair) * (1.0 / HEAD_DIM)
        yn = dlt * lax.rsqrt(var + GN_EPS) * gnw_ref[...] + gnb_ref[...]
        g = g_ref[0, rows, :]
        out_ref[0, rows, :] = ((yn + bonus_refs[ck][...]) * (g * _sigmoid(g))).astype(out_ref.dtype)
    yield


_STAGE_ORDER = "FSSFSSSF"


def _rwkv_kernel(ps_ref, prev_ref, g_ref, mu_ref, w0_ref, wup_ref, a0_ref, aup_ref,
                 kkw_ref, ka_ref, rk_ref, gnw_ref, gnb_ref, ones_ref, tri_ref,
                 out_ref, st_ref, *chunk_scr):
    s = pl.program_id(1)
    sets = [chunk_scr[i * N_CHUNK_SCRATCH:(i + 1) * N_CHUNK_SCRATCH] for i in range(2 * CHUNKS_PER_STEP)]
    slot_a, slot_b = sets[:CHUNKS_PER_STEP], sets[CHUNKS_PER_STEP:]

    @pl.when(s == 0)
    def _():
        st_ref[...] = jnp.zeros_like(st_ref)
        for scr in slot_b:
            for ref in scr:
                ref[...] = jnp.zeros_like(ref)

    params = (mu_ref, w0_ref, wup_ref, a0_ref, aup_ref, kkw_ref, ka_ref, rk_ref)

    def step(fe_scrs, sv_scrs):
        fes = []
        for ck in range(CHUNKS_PER_STEP):
            if ck == 0:
                prev_last = prev_ref[0, 7:8, :] * jnp.where(s > 0, 1.0, 0.0)
            else:
                prev_last = ps_ref[0, ck * CHUNK - 1:ck * CHUNK, :]
            fes.append(_frontend(ps_ref, ck, prev_last, params, ones_ref[...], tri_ref[...], fe_scrs[ck]))
        sv = _solver(g_ref, gnw_ref, gnb_ref, ones_ref[...], out_ref, st_ref, sv_scrs)
        for key in _STAGE_ORDER:
            for gen in (fes if key == "F" else [sv]):
                next(gen)
        for gen in [sv] + fes:
            for _ in gen:
                pass

    @pl.when((s & 1) == 0)
    def _():
        step(slot_a, slot_b)

    @pl.when((s & 1) == 1)
    def _():
        step(slot_b, slot_a)


def _rwkv(ps, g_r, mu, w0, wup_pad, a0, aup_pad, kkw, ka, rk, gnw, gnb, ones_pair, tri):
    bsz, seq, _ = ps.shape
    ns = seq // STEP_ROWS
    sub = STEP_ROWS // 8

    def vec(width):
        return pl.BlockSpec((1, width), lambda b, s: (0, 0))

    def fe_block(s):
        return jnp.minimum(s, ns - 1)

    def sv_block(s):
        return jnp.maximum(s - 1, 0)

    return pl.pallas_call(
        _rwkv_kernel,
        grid=(bsz, ns + 1),
        in_specs=[
            pl.BlockSpec((1, STEP_ROWS, SHIFT_WIDTH), lambda b, s: (b, fe_block(s), 0)),
            pl.BlockSpec((1, 8, SHIFT_WIDTH), lambda b, s: (b, jnp.maximum(fe_block(s) * sub - 1, 0), 0)),
            pl.BlockSpec((1, STEP_ROWS, RWKV_WIDTH), lambda b, s: (b, sv_block(s), 0)),
            vec(SHIFT_WIDTH),
            vec(RWKV_WIDTH),
            pl.BlockSpec((LANES, RWKV_WIDTH), lambda b, s: (0, 0)),
            vec(RWKV_WIDTH),
            pl.BlockSpec((LANES, RWKV_WIDTH), lambda b, s: (0, 0)),
            vec(RWKV_WIDTH), vec(RWKV_WIDTH), vec(RWKV_WIDTH), vec(RWKV_WIDTH), vec(RWKV_WIDTH),
            pl.BlockSpec((LANES, LANES), lambda b, s: (0, 0)),
            pl.BlockSpec((CHUNK, CHUNK), lambda b, s: (0, 0)),
        ],
        out_specs=pl.BlockSpec((1, STEP_ROWS, RWKV_WIDTH), lambda b, s: (b, sv_block(s), 0)),
        out_shape=jax.ShapeDtypeStruct((bsz, seq, RWKV_WIDTH), BF16),
        scratch_shapes=[pltpu.VMEM((PAIRS, LANES, LANES), F32)]
        + [buf for _ in range(2 * CHUNKS_PER_STEP) for buf in _chunk_scratch()],
        compiler_params=pltpu.CompilerParams(
            dimension_semantics=("arbitrary", "arbitrary"), vmem_limit_bytes=VMEM_LIMIT),
        name="rwkv7_chunked",
    )(ps, ps, g_r, mu, w0, wup_pad, a0, aup_pad, kkw, ka, rk, gnw, gnb, ones_pair, tri)


def _attn_kernel(sinks_ref, q_ref, kvc_ref, kvp_ref, g_ref, o_ref):
    n = pl.program_id(1)
    q = q_ref[0]
    kvc = kvc_ref[0]
    kvp = kvp_ref[0]
    kw_t = jnp.concatenate([kvp[:, :KV_WIDTH], kvc[:, :KV_WIDTH]], axis=0).T.astype(BF16)
    vw = jnp.concatenate([kvp[:, KV_WIDTH:], kvc[:, KV_WIDTH:]], axis=0).astype(BF16)
    qi = lax.broadcasted_iota(jnp.int32, (WINDOW, 2 * WINDOW), 0)
    si = lax.broadcasted_iota(jnp.int32, (WINDOW, 2 * WINDOW), 1)
    first_valid = jnp.where(n > 0, 0, WINDOW)
    mask = (si > qi) & (si <= qi + WINDOW) & (si >= first_valid)
    lo = lax.broadcasted_iota(jnp.int32, (WINDOW, LANES), 1) < HEAD_DIM
    scale = HEAD_DIM ** -0.5
    outs = []
    for blk in range(ATTN_WIDTH // LANES):
        qb = q[:, blk * LANES:(blk + 1) * LANES]
        halves = []
        for half in range(2):
            own = lo if half == 0 else jnp.logical_not(lo)
            s = jnp.dot(jnp.where(own, qb, 0.0).astype(BF16), kw_t, preferred_element_type=F32) * scale
            s = jnp.where(mask, s, -1e30)
            sink = sinks_ref[blk + 4 * half]
            m = jnp.maximum(jnp.max(s, axis=-1, keepdims=True), sink)
            p = jnp.exp(s - m)
            denom = jnp.sum(p, axis=-1, keepdims=True) + jnp.exp(sink - m)
            halves.append(jnp.dot(p.astype(BF16), vw, preferred_element_type=F32) / denom)
        outs.append(jnp.where(lo, halves[0], halves[1]))
    o = jnp.concatenate(outs, axis=1)
    g = g_ref[0]
    o_ref[0] = (o * (g * _sigmoid(g))).astype(o_ref.dtype)


def _attn(sinks, q, kv, g_a):
    bsz, seq, _ = q.shape
    nb = seq // WINDOW
    return pl.pallas_call(
        _attn_kernel,
        grid=(bsz, nb),
        in_specs=[
            pl.BlockSpec(memory_space=pltpu.SMEM),
            pl.BlockSpec((1, WINDOW, ATTN_WIDTH), lambda b, n: (b, n, 0)),
            pl.BlockSpec((1, WINDOW, 2 * KV_WIDTH), lambda b, n: (b, n, 0)),
            pl.BlockSpec((1, WINDOW, 2 * KV_WIDTH), lambda b, n: (b, jnp.maximum(n - 1, 0), 0)),
            pl.BlockSpec((1, WINDOW, ATTN_WIDTH), lambda b, n: (b, n, 0)),
        ],
        out_specs=pl.BlockSpec((1, WINDOW, ATTN_WIDTH), lambda b, n: (b, n, 0)),
        out_shape=jax.ShapeDtypeStruct((bsz, seq, ATTN_WIDTH), BF16),
        compiler_params=pltpu.CompilerParams(
            dimension_semantics=("arbitrary", "arbitrary"), vmem_limit_bytes=VMEM_LIMIT),
        name="swa_sink",
    )(sinks, q, kv, kv, g_a)


def _outproj_kernel(ya_ref, yb_ref, x_ref, w_ref, fg_ref, o_ref, *, final):
    y = (jnp.dot(ya_ref[...], w_ref[:RWKV_WIDTH, :], preferred_element_type=F32)
         + jnp.dot(yb_ref[...], w_ref[RWKV_WIDTH:, :], preferred_element_type=F32))
    xn = x_ref[...] + y
    if final:
        ms = jnp.mean(xn * xn, axis=-1, keepdims=True)
        xn = xn * lax.rsqrt(ms + NORM_EPS) * fg_ref[...]
    o_ref[...] = xn


def _outproj(ya, yb, x2, w_bf16, fgain, final):
    rows = x2.shape[0]
    tm = ROW_TILE_OUT
    return pl.pallas_call(
        functools.partial(_outproj_kernel, final=final),
        grid=(rows // tm,),
        in_specs=[
            pl.BlockSpec((tm, RWKV_WIDTH), lambda i: (i, 0)),
            pl.BlockSpec((tm, ATTN_WIDTH), lambda i: (i, 0)),
            pl.BlockSpec((tm, D_MODEL), lambda i: (i, 0)),
            pl.BlockSpec((D_MODEL, D_MODEL), lambda i: (0, 0)),
            pl.BlockSpec((1, D_MODEL), lambda i: (0, 0)),
        ],
        out_specs=pl.BlockSpec((tm, D_MODEL), lambda i: (i, 0)),
        out_shape=jax.ShapeDtypeStruct((rows, D_MODEL), F32),
        compiler_params=pltpu.CompilerParams(
            dimension_semantics=("arbitrary",), vmem_limit_bytes=VMEM_LIMIT),
        name="outproj_final" if final else "outproj",
    )(ya, yb, x2, w_bf16, fgain)


_HEAD_ORDER = np.array([0, 4, 1, 5, 2, 6, 3, 7])
_ATTN_PERM = (_HEAD_ORDER[:, None] * HEAD_DIM + np.arange(HEAD_DIM)[None, :]).reshape(-1)


def kernel(x, norm_gain, w_in, shift_mu, w0, w_up, a0, a_up, k_k, k_a, r_k, gn_w, gn_b, sinks, w_out, final_gain):
    bsz, seq, _ = x.shape
    depth = w_in.shape[0]
    rows = bsz * seq

    q0 = SHIFT_WIDTH + RWKV_WIDTH
    ga0 = q0 + ATTN_WIDTH + 2 * KV_WIDTH
    col_perm = np.arange(IN_WIDTH)
    col_perm[q0:q0 + ATTN_WIDTH] = q0 + _ATTN_PERM
    col_perm[ga0:ga0 + ATTN_WIDTH] = ga0 + _ATTN_PERM
    row_perm = np.arange(D_MODEL)
    row_perm[RWKV_WIDTH:] = RWKV_WIDTH + _ATTN_PERM

    head_id = np.arange(LANES) // HEAD_DIM
    ones_pair = jnp.asarray(head_id[:, None] == head_id[None, :], dtype=BF16)
    tri = jnp.asarray(np.tril(np.ones((CHUNK, CHUNK), np.float32)), dtype=BF16)
    zpad = jnp.zeros((LORA, RWKV_WIDTH), F32)

    x2 = x.reshape(rows, D_MODEL)
    for l in range(depth):
        w_in_l = w_in[l][:, col_perm].astype(BF16)
        w_out_l = w_out[l][row_perm, :].astype(BF16)
        wup_pad = jnp.concatenate([w_up[l], zpad], axis=0).astype(BF16)
        aup_pad = jnp.concatenate([zpad, a_up[l]], axis=0).astype(BF16)

        ps, g_r, q, kv, g_a = _inproj(x2, norm_gain[l][None, :], w_in_l)
        y_a = _rwkv(ps.reshape(bsz, seq, SHIFT_WIDTH), g_r.reshape(bsz, seq, RWKV_WIDTH),
                    shift_mu[l][None, :], w0[l][None, :], wup_pad, a0[l][None, :], aup_pad,
                    k_k[l][None, :], k_a[l][None, :], r_k[l].reshape(1, RWKV_WIDTH),
                    gn_w[l][None, :], gn_b[l][None, :], ones_pair, tri)
        y_b = _attn(sinks[l], q.reshape(bsz, seq, ATTN_WIDTH), kv.reshape(bsz, seq, 2 * KV_WIDTH),
                    g_a.reshape(bsz, seq, ATTN_WIDTH))
        x2 = _outproj(y_a.reshape(rows, RWKV_WIDTH), y_b.reshape(rows, ATTN_WIDTH), x2, w_out_l,
                      final_gain[None, :], final=(l == depth - 1))
    return x2.reshape(bsz, seq, D_MODEL)
```

```python
import functools

import numpy as np
import jax
import jax.numpy as jnp
from jax import lax
from jax.experimental import pallas as pl
from jax.experimental.pallas import tpu as pltpu

D_MODEL = 1024
HEAD_DIM = 64
RWKV_WIDTH = 512
ATTN_WIDTH = 512
KV_WIDTH = 128
LORA = 64
SHIFT_WIDTH = 3 * RWKV_WIDTH + 2 * LORA
IN_WIDTH = SHIFT_WIDTH + RWKV_WIDTH + ATTN_WIDTH + 2 * KV_WIDTH + ATTN_WIDTH
NORM_EPS = 1e-5
GN_EPS = 64e-5
WINDOW = 128
CHUNK = 128
LANES = 128
PAIRS = RWKV_WIDTH // LANES
CHUNKS_PER_STEP = 2
STEP_ROWS = CHUNKS_PER_STEP * CHUNK
ATTN_Q_BLOCKS = 4
ATTN_ROWS = ATTN_Q_BLOCKS * WINDOW
LOG2E = 1.4426950408889634
ROW_TILE_IN = 256
ROW_TILE_OUT = 512
VMEM_LIMIT = 48 * 1024 * 1024

F32 = jnp.float32
BF16 = jnp.bfloat16


def _mm(a, b):
    return jnp.dot(a.astype(BF16), b.astype(BF16), preferred_element_type=F32)


def _sigmoid(x):
    return 1.0 / (1.0 + jnp.exp(-x))


def _inproj_kernel(x_ref, gain_ref, w_ref, ps_ref, gr_ref, q_ref, kv_ref, ga_ref):
    x = x_ref[...]
    ms = jnp.mean(x * x, axis=-1, keepdims=True)
    h = (x * lax.rsqrt(ms + NORM_EPS) * gain_ref[...]).astype(BF16)
    c0 = 0
    for ref in (ps_ref, gr_ref, q_ref, kv_ref, ga_ref):
        w = ref.shape[-1]
        ref[...] = jnp.dot(h, w_ref[:, c0:c0 + w], preferred_element_type=F32)
        c0 += w


def _inproj(x2, gain, w_bf16):
    rows = x2.shape[0]
    tm = ROW_TILE_IN
    widths = (SHIFT_WIDTH, RWKV_WIDTH, ATTN_WIDTH, 2 * KV_WIDTH, ATTN_WIDTH)
    return pl.pallas_call(
        _inproj_kernel,
        grid=(rows // tm,),
        in_specs=[
            pl.BlockSpec((tm, D_MODEL), lambda i: (i, 0)),
            pl.BlockSpec((1, D_MODEL), lambda i: (0, 0)),
            pl.BlockSpec((D_MODEL, IN_WIDTH), lambda i: (0, 0)),
        ],
        out_specs=[pl.BlockSpec((tm, w), lambda i: (i, 0)) for w in widths],
        out_shape=[jax.ShapeDtypeStruct((rows, w), F32) for w in widths],
        compiler_params=pltpu.CompilerParams(
            dimension_semantics=("arbitrary",), vmem_limit_bytes=VMEM_LIMIT),
        name="inproj",
    )(x2, gain, w_bf16)


def _head_sum(x, ones_pair):
    return jnp.concatenate(
        [_mm(x[:, pr * LANES:(pr + 1) * LANES], ones_pair) for pr in range(PAIRS)], axis=1)


def _swap_pairs(x):
    return jnp.concatenate(
        [pltpu.roll(x[:, pr * LANES:(pr + 1) * LANES], HEAD_DIM, 1) for pr in range(PAIRS)], axis=1)


def _chunk_scratch():
    return [
        pltpu.VMEM((2, 2 * CHUNK, RWKV_WIDTH), BF16),
        pltpu.VMEM((RWKV_WIDTH, 2 * CHUNK), BF16),
        pltpu.VMEM((2 * RWKV_WIDTH, CHUNK), BF16),
        pltpu.VMEM((CHUNK, RWKV_WIDTH), BF16),
        pltpu.VMEM((2, CHUNK, RWKV_WIDTH), BF16),
        pltpu.VMEM((CHUNK, RWKV_WIDTH), F32),
        pltpu.VMEM((CHUNK, RWKV_WIDTH), F32),
        pltpu.VMEM((CHUNK, RWKV_WIDTH), F32),
        pltpu.VMEM((8, RWKV_WIDTH), F32),
    ]


N_CHUNK_SCRATCH = len(_chunk_scratch())


def _frontend(ps_ref, ck, prev_last, params, ones_pair, tri, scr):
    mu_ref, w0_ref, wup_ref, a0_ref, aup_ref, kkw_ref, ka_ref, rk_ref = params
    lhs_ref, rhs_ref, ht_ref, vsw_ref, vso_ref, as_ref, rs_ref, bonus_ref, gam_ref = scr

    p = ps_ref[0, ck * CHUNK:(ck + 1) * CHUNK, :]
    rows1 = lax.broadcasted_iota(jnp.int32, (CHUNK, 1), 0)
    prev =jnp.where(rows1 == 0, prev_last, pltpu.roll(p, 1, 0))
    sh = p + (prev - p) * mu_ref[...]
    r = sh[:, 0:RWKV_WIDTH]
    k = sh[:, RWKV_WIDTH:2 * RWKV_WIDTH]
    v = sh[:, 2 * RWKV_WIDTH:3 * RWKV_WIDTH]
    wa = sh[:, 3 * RWKV_WIDTH:]
    lw = _mm(jnp.tanh(wa), wup_ref[...])
    la = _mm(wa, aup_ref[...])
    kkraw = k * kkw_ref[...]
    n2 = _head_sum(kkraw * kkraw, ones_pair)
    yield

    z = w0_ref[...] + lw
    wlog = -(jnp.maximum(-z, 0.0) + jnp.log(1.0 + jnp.exp(-jnp.abs(z)))) - 0.5
    ld = -jnp.exp(wlog)
    a = _sigmoid(a0_ref[...] + la)
    kk = kkraw / jnp.maximum(jnp.sqrt(n2), 1e-12)
    kp = k * (1.0 + (a - 1.0) * ka_ref[...])
    bv = kk * a
    bonus_ref[...] = _head_sum(r * kp * rk_ref[...], ones_pair) * v
    l_hi = ld.astype(BF16)
    rem = ld - l_hi.astype(F32)
    l_mid = rem.astype(BF16)
    l_lo = (rem - l_mid.astype(F32)).astype(BF16)
    cs = (jnp.dot(tri, l_hi, preferred_element_type=F32)
          + jnp.dot(tri, l_mid, preferred_element_type=F32)
          + jnp.dot(tri, l_lo, preferred_element_type=F32))
    yield

    mid = cs[CHUNK // 2 - 1:CHUNK // 2, :]
    end = cs[CHUNK - 1:CHUNK, :]
    e_out = jnp.exp(mid - cs)
    e_end = jnp.exp(end - cs)
    a_t = -kk * jnp.exp(cs - ld - mid)
    r_t = r * jnp.exp(cs - mid)
    g_mid = jnp.exp(mid)
    as_ref[...] = a_t * g_mid
    rs_ref[...] = r_t * g_mid
    gam_ref[...] = jnp.broadcast_to(jnp.exp(end), gam_ref.shape)
    lo = (lax.broadcasted_iota(jnp.int32, (CHUNK, RWKV_WIDTH), 1) & HEAD_DIM) == 0
    lhs_ref[0, :CHUNK, :] = jnp.where(lo, a_t, 0.0).astype(BF16)
    lhs_ref[0, CHUNK:, :] = jnp.where(lo, r_t, 0.0).astype(BF16)
    lhs_ref[1, :CHUNK, :] = jnp.where(lo, 0.0, a_t).astype(BF16)
    lhs_ref[1, CHUNK:, :] = jnp.where(lo, 0.0, r_t).astype(BF16)
    rhs_ref[:, :CHUNK] = (bv * e_out).T.astype(BF16)
    rhs_ref[:, CHUNK:] = (kp * e_out).T.astype(BF16)
    ht_ref[:RWKV_WIDTH, :] = (bv * e_end).T.astype(BF16)
    ht_ref[RWKV_WIDTH:, :] = (kp * e_end).T.astype(BF16)
    v_sw = _swap_pairs(v)
    vsw_ref[...] = v_sw.astype(BF16)
    vso_ref[0] = jnp.where(lo, 0.0, v_sw).astype(BF16)
    vso_ref[1] = jnp.where(lo, v_sw, 0.0).astype(BF16)
    yield


def _solver(g_ref, gnw_ref, gnb_ref, ones_pair, out_ref, st_ref, scrs):
    row = lax.broadcasted_iota(jnp.int32, (CHUNK, CHUNK), 0)
    col = lax.broadcasted_iota(jnp.int32, (CHUNK, CHUNK), 1)
    strict = col < row
    incl2 = jnp.concatenate([col <= row, col <= row], axis=1)
    lo = col < HEAD_DIM
    owns = [lo, jnp.logical_not(lo)]
    same_head = (row >> 6) == (col >> 6)
    eye = row == col
    blk8 = (row >> 3) == (col >> 3)
    n_ck = len(scrs)
    heads = [(ck, pr, hh) for ck in range(n_ck) for pr in range(PAIRS) for hh in range(2)]
    sls = [slice(pr * LANES, (pr + 1) * LANES) for pr in range(PAIRS)]
    lhs_refs, rhs_refs, ht_refs, vsw_refs, vso_refs, as_refs, rs_refs, bonus_refs, gam_refs = zip(*scrs)

    amats = [jnp.dot(lhs_refs[ck][hh, :, sls[pr]], rhs_refs[ck][sls[pr], :], preferred_element_type=F32)
             for ck, pr, hh in heads]
    yield
    n_mats = [jnp.where(strict, am[:CHUNK, :CHUNK], 0.0) for am in amats]
    avs = [jnp.dot(jnp.where(strict, am[:CHUNK, CHUNK:], 0.0).astype(BF16), vsw_refs[ck][:, sls[pr]],
                   preferred_element_type=F32) for (ck, pr, hh), am in zip(heads, amats)]
    nds = [jnp.where(blk8, n, 0.0) for n in n_mats]
    ndbs = [nd.astype(BF16) for nd in nds]
    s2s = [jnp.dot(ndb, ndb, preferred_element_type=F32) for ndb in ndbs]
    yield
    ps = [jnp.where(eye, 1.0, nd) for nd in nds]
    pss = [_mm(jnp.concatenate([p, s2], axis=0), s2) for p, s2 in zip(ps, s2s)]
    yield
    ps = [p + x[:CHUNK] for p, x in zip(ps, pss)]
    dbs = [(p + _mm(p, x[CHUNK:])).astype(BF16) for p, x in zip(ps, pss)]
    yield
    for sh in (3, 4, 5):
        off = ((row >> (sh + 1)) == (col >> (sh + 1))) & ((row >> sh) != (col >> sh))
        ts = [jnp.dot(jnp.where(off, n, 0.0).astype(BF16), db, preferred_element_type=F32)
              for n, db in zip(n_mats, dbs)]
        yield
        dbs = [db + jnp.dot(db, t.astype(BF16), preferred_element_type=F32).astype(BF16) for db, t in zip(dbs, ts)]
        yield
    half = CHUNK // 2
    y0s = [jnp.dot(db, jnp.where(owns[hh], as_refs[ck][:, sls[pr]], av).astype(BF16), preferred_element_type=F32)
           for (ck, pr, hh), db, av in zip(heads, dbs, avs)]
    yield
    y0bs = [y0.astype(BF16) for y0 in y0s]
    lo_half = lax.broadcasted_iota(jnp.int32, (half, CHUNK), 1) < half
    ws = [jnp.dot(jnp.where(lo_half, n[half:], 0.0).astype(BF16), y0b, preferred_element_type=F32)
          for n, y0b in zip(n_mats, y0bs)]
    yield
    xss = [jnp.concatenate(
        [y0b[:half],
         (y0[half:] + jnp.dot(db[half:], jnp.concatenate([y0b[:half], w.astype(BF16)], axis=0),
                              preferred_element_type=F32)).astype(BF16)], axis=0)
        for y0, y0b, db, w in zip(y0s, y0bs, dbs, ws)]
    yield
    zs = [_mm(jnp.where(incl2, am[CHUNK:, :], 0.0),
              jnp.concatenate([xs, vso_refs[ck][hh, :, sls[pr]]], axis=0))
          for (ck, pr, hh), am, xs in zip(heads, amats, xss)]
    res = [jnp.dot(ht_refs[ck][sls[pr], :], xs, preferred_element_type=F32)
           for (ck, pr, hh), xs in zip(heads, xss)]
    kvs = [[jnp.dot(ht_refs[ck][RWKV_WIDTH + pr * LANES:RWKV_WIDTH + (pr + 1) * LANES, :],
                    vsw_refs[ck][:, sls[pr]], preferred_element_type=F32) for pr in range(PAIRS)]
           for ck in range(n_ck)]
    yield
    ys = [[] for _ in range(n_ck)]
    for pr in range(PAIRS):
        sl = sls[pr]
        st = st_ref[pr]
        for ck in range(n_ck):
            i0 = (ck * PAIRS + pr) * 2
            z0, z1 = zs[i0], zs[i0 + 1]
            cat = jnp.concatenate([res[i0][:HEAD_DIM], res[i0 + 1][HEAD_DIM:]], axis=0)
            m_t = jnp.where(same_head, cat, 0.0) + jnp.where(eye, gam_refs[ck][0:1, sl], 0.0)
            g_t = jnp.where(same_head, 0.0, cat + kvs[ck][pr])
            q_hat = rs_refs[ck][:, sl] + jnp.where(lo, z0, z1)
            y_loc_sw = jnp.where(lo, z1, z0)
            stb = st.astype(BF16)
            ys[ck].append(_mm(q_hat, stb) + y_loc_sw)
            st = _mm(m_t, stb) + g_t
        st_ref[pr] = st
    yield
    for ck in range(n_ck):
        rows = slice(ck * CHUNK, (ck + 1) * CHUNK)
        y = _swap_pairs(jnp.concatenate(ys[ck], axis=1))
        mean = _head_sum(y, ones_pair) * (1.0 / HEAD_DIM)
        dlt = y - mean
        var = _head_sum(dlt * dlt, ones_pair) * (1.0 / HEAD_DIM)
        yn = dlt * lax.rsqrt(var + GN_EPS) * gnw_ref[...] + gnb_ref[...]
        g = g_ref[0, rows, :]
        out_ref[0, rows, :] = ((yn + bonus_refs[ck][...]) * (g * _sigmoid(g))).astype(out_ref.dtype)
    yield


_STAGE_ORDER = "FSSFSSSF"


def _rwkv_kernel(ps_ref, prev_ref, g_ref, mu_ref, w0_ref, wup_ref, a0_ref, aup_ref,
                 kkw_ref, ka_ref, rk_ref, gnw_ref, gnb_ref, ones_ref, tri_ref,
                 out_ref, st_ref, *chunk_scr):
    s = pl.program_id(1)
    sets = [chunk_scr[i * N_CHUNK_SCRATCH:(i + 1) * N_CHUNK_SCRATCH] for i in range(2 * CHUNKS_PER_STEP)]
    slot_a, slot_b = sets[:CHUNKS_PER_STEP], sets[CHUNKS_PER_STEP:]

    @pl.when(s == 0)
    def _():
        st_ref[...] = jnp.zeros_like(st_ref)
        for scr in slot_b:
            for ref in scr:
                ref[...] = jnp.zeros_like(ref)

    params = (mu_ref, w0_ref, wup_ref, a0_ref, aup_ref, kkw_ref, ka_ref, rk_ref)

    def step(fe_scrs, sv_scrs):
        fes = []
        for ck in range(CHUNKS_PER_STEP):
            if ck == 0:
                prev_last = prev_ref[0, 7:8, :] * jnp.where(s > 0, 1.0, 0.0)
            else:
                prev_last = ps_ref[0, ck * CHUNK - 1:ck * CHUNK, :]
            fes.append(_frontend(ps_ref, ck, prev_last, params, ones_ref[...], tri_ref[...], fe_scrs[ck]))
        sv = _solver(g_ref, gnw_ref, gnb_ref, ones_ref[...], out_ref, st_ref, sv_scrs)
        for key in _STAGE_ORDER:
            for gen in (fes if key == "F" else [sv]):
                next(gen)
        for gen in [sv] + fes:
            for _ in gen:
                pass

    @pl.when((s & 1) == 0)
    def _():
        step(slot_a, slot_b)

    @pl.when((s & 1) == 1)
    def _():
        step(slot_b, slot_a)


def _rwkv(ps, g_r, mu, w0, wup_pad, a0, aup_pad, kkw, ka, rk, gnw, gnb, ones_pair, tri):
    bsz, seq, _ = ps.shape
    ns = seq // STEP_ROWS
    sub = STEP_ROWS // 8

    def vec(width):
        return pl.BlockSpec((1, width), lambda b, s: (0, 0))

    def fe_block(s):
        return jnp.minimum(s, ns - 1)

    def sv_block(s):
        return jnp.maximum(s - 1, 0)

    return pl.pallas_call(
        _rwkv_kernel,
        grid=(bsz, ns + 1),
        in_specs=[
            pl.BlockSpec((1, STEP_ROWS, SHIFT_WIDTH), lambda b, s: (b, fe_block(s), 0)),
            pl.BlockSpec((1, 8, SHIFT_WIDTH), lambda b, s: (b, jnp.maximum(fe_block(s) * sub - 1, 0), 0)),
            pl.BlockSpec((1, STEP_ROWS, RWKV_WIDTH), lambda b, s: (b, sv_block(s), 0)),
            vec(SHIFT_WIDTH),
            vec(RWKV_WIDTH),
            pl.BlockSpec((LANES, RWKV_WIDTH), lambda b, s: (0, 0)),
            vec(RWKV_WIDTH),
            pl.BlockSpec((LANES, RWKV_WIDTH), lambda b, s: (0, 0)),
            vec(RWKV_WIDTH), vec(RWKV_WIDTH), vec(RWKV_WIDTH), vec(RWKV_WIDTH), vec(RWKV_WIDTH),
            pl.BlockSpec((LANES, LANES), lambda b, s: (0, 0)),
            pl.BlockSpec((CHUNK, CHUNK), lambda b, s: (0, 0)),
        ],
        out_specs=pl.BlockSpec((1, STEP_ROWS, RWKV_WIDTH), lambda b, s: (b, sv_block(s), 0)),
        out_shape=jax.ShapeDtypeStruct((bsz, seq, RWKV_WIDTH), BF16),
        scratch_shapes=[pltpu.VMEM((PAIRS, LANES, LANES), F32)]
        + [buf for _ in range(2 * CHUNKS_PER_STEP) for buf in _chunk_scratch()],
        compiler_params=pltpu.CompilerParams(
            dimension_semantics=("arbitrary", "arbitrary"), vmem_limit_bytes=VMEM_LIMIT),
        name="rwkv7_chunked",
    )(ps, ps, g_r, mu, w0, wup_pad, a0, aup_pad, kkw, ka, rk, gnw, gnb, ones_pair, tri)


def _attn_kernel(sinks_ref, q_ref, kvc_ref, kvp_ref, g_ref, o_ref):
    n = pl.program_id(1)
    kv = jnp.concatenate([kvp_ref[0], kvc_ref[0]], axis=0)
    k_nat = kv[:, :KV_WIDTH]
    v_nat = kv[:, KV_WIDTH:]
    ks = [k_nat.astype(BF16), pltpu.roll(k_nat, HEAD_DIM, 1).astype(BF16)]
    vts = [v_nat.T.astype(BF16), pltpu.roll(v_nat, HEAD_DIM, 1).T.astype(BF16)]
    si = lax.broadcasted_iota(jnp.int32, (2 * WINDOW, WINDOW), 0)
    qi = lax.broadcasted_iota(jnp.int32, (2 * WINDOW, WINDOW), 1)
    band = (si > qi) & (si <= qi + WINDOW)
    first_valid = jnp.where(n > 0, 0, WINDOW)
    lo = lax.broadcasted_iota(jnp.int32, (WINDOW, LANES), 1) < HEAD_DIM
    owns = [lo, jnp.logical_not(lo)]
    qs = q_ref[0] * (HEAD_DIM ** -0.5 * LOG2E)
    n_blk = ATTN_WIDTH // LANES
    group = n_blk // 2

    def scores(j):
        rows = slice(j * WINDOW, (j + 1) * WINDOW)
        keys = slice(j * WINDOW, (j + 2) * WINDOW)
        out = []
        for blk in range(n_blk):
            qb = qs[rows, blk * LANES:(blk + 1) * LANES]
            for hf in range(2):
                kk = ks[0 if blk // group == hf else 1][keys]
                out.append(lax.dot_general(kk, jnp.where(owns[hf], qb, 0.0).astype(BF16),
                                           (((1,), (1,)), ((), ())), preferred_element_type=F32))
        return out

    def finish(j, sc):
        rows = slice(j * WINDOW, (j + 1) * WINDOW)
        keys = slice(j * WINDOW, (j + 2) * WINDOW)
        mask = band & (si >= first_valid) if j == 0 else band
        outs = []
        for blk in range(n_blk):
            halves = []
            for hf in range(2):
                s = jnp.where(mask, sc[2 * blk + hf], -1e30)
                sink = sinks_ref[2 * blk + hf] * LOG2E
                m = jnp.maximum(jnp.max(s, axis=0, keepdims=True), sink)
                p = jnp.exp2(s - m)
                denom = jnp.sum(p, axis=0, keepdims=True) + jnp.exp2(sink - m)
                vt = vts[0 if blk // group == hf else 1][:, keys]
                halves.append(jnp.dot(vt, p.astype(BF16), preferred_element_type=F32) / denom)
            outs.append(jnp.concatenate([halves[0][:HEAD_DIM], halves[1][HEAD_DIM:]], axis=0).T)
        g = g_ref[0, rows, :]
        o_ref[0, rows, :] = (jnp.concatenate(outs, axis=1) * (g * _sigmoid(g))).astype(o_ref.dtype)

    sc = scores(0)
    for j in range(ATTN_Q_BLOCKS):
        nxt = scores(j + 1) if j + 1 < ATTN_Q_BLOCKS else None
        finish(j, sc)
        sc = nxt


def _attn(sinks, q, kv, g_a):
    bsz, seq, _ = q.shape
    nb = seq // ATTN_ROWS
    return pl.pallas_call(
        _attn_kernel,
        grid=(bsz, nb),
        in_specs=[
            pl.BlockSpec(memory_space=pltpu.SMEM),
            pl.BlockSpec((1, ATTN_ROWS, ATTN_WIDTH), lambda b, n: (b, n, 0)),
            pl.BlockSpec((1, ATTN_ROWS, 2 * KV_WIDTH), lambda b, n: (b, n, 0)),
            pl.BlockSpec((1, WINDOW, 2 * KV_WIDTH), lambda b, n: (b, jnp.maximum(n * ATTN_Q_BLOCKS - 1, 0), 0)),
            pl.BlockSpec((1, ATTN_ROWS, ATTN_WIDTH), lambda b, n: (b, n, 0)),
        ],
        out_specs=pl.BlockSpec((1, ATTN_ROWS, ATTN_WIDTH), lambda b, n: (b, n, 0)),
        out_shape=jax.ShapeDtypeStruct((bsz, seq, ATTN_WIDTH), BF16),
        compiler_params=pltpu.CompilerParams(
            dimension_semantics=("arbitrary", "arbitrary"), vmem_limit_bytes=VMEM_LIMIT),
        name="swa_sink",
    )(sinks, q, kv, kv, g_a)


def _outproj_kernel(ya_ref, yb_ref, x_ref, w_ref, fg_ref, o_ref, *, final):
    y = (jnp.dot(ya_ref[...], w_ref[:RWKV_WIDTH, :], preferred_element_type=F32)
         + jnp.dot(yb_ref[...], w_ref[RWKV_WIDTH:, :], preferred_element_type=F32))
    xn = x_ref[...] + y
    if final:
        ms = jnp.mean(xn * xn, axis=-1, keepdims=True)
        xn = xn * lax.rsqrt(ms + NORM_EPS) * fg_ref[...]
    o_ref[...] = xn


def _outproj(ya, yb, x2, w_bf16, fgain, final):
    rows = x2.shape[0]
    tm = ROW_TILE_OUT
    return pl.pallas_call(
        functools.partial(_outproj_kernel, final=final),
        grid=(rows // tm,),
        in_specs=[
            pl.BlockSpec((tm, RWKV_WIDTH), lambda i: (i, 0)),
            pl.BlockSpec((tm, ATTN_WIDTH), lambda i: (i, 0)),
            pl.BlockSpec((tm, D_MODEL), lambda i: (i, 0)),
            pl.BlockSpec((D_MODEL, D_MODEL), lambda i: (0, 0)),
            pl.BlockSpec((1, D_MODEL), lambda i: (0, 0)),
        ],
        out_specs=pl.BlockSpec((tm, D_MODEL), lambda i: (i, 0)),
        out_shape=jax.ShapeDtypeStruct((rows, D_MODEL), F32),
        compiler_params=pltpu.CompilerParams(
            dimension_semantics=("arbitrary",), vmem_limit_bytes=VMEM_LIMIT),
        name="outproj_final" if final else "outproj",
    )(ya, yb, x2, w_bf16, fgain)


def kernel(x, norm_gain, w_in, shift_mu, w0, w_up, a0, a_up, k_k, k_a, r_k, gn_w, gn_b, sinks, w_out, final_gain):
    bsz, seq, _ = x.shape
    depth = w_in.shape[0]
    rows = bsz * seq

    head_id = np.arange(LANES) // HEAD_DIM
    ones_pair = jnp.asarray(head_id[:, None] == head_id[None, :], dtype=BF16)
    tri = jnp.asarray(np.tril(np.ones((CHUNK, CHUNK), np.float32)), dtype=BF16)
    zpad = jnp.zeros((LORA, RWKV_WIDTH), F32)

    x2 = x.reshape(rows, D_MODEL)
    for l in range(depth):
        w_in_l = w_in[l].astype(BF16)
        w_out_l = w_out[l].astype(BF16)
        wup_pad = jnp.concatenate([w_up[l], zpad], axis=0).astype(BF16)
        aup_pad = jnp.concatenate([zpad, a_up[l]], axis=0).astype(BF16)

        ps, g_r, q, kv, g_a = _inproj(x2, norm_gain[l][None, :], w_in_l)
        y_a = _rwkv(ps.reshape(bsz, seq, SHIFT_WIDTH), g_r.reshape(bsz, seq, RWKV_WIDTH),
                    shift_mu[l][None, :], w0[l][None, :], wup_pad, a0[l][None, :], aup_pad,
                    k_k[l][None, :], k_a[l][None, :], r_k[l].reshape(1, RWKV_WIDTH),
                    gn_w[l][None, :], gn_b[l][None, :], ones_pair, tri)
        y_b = _attn(sinks[l], q.reshape(bsz, seq, ATTN_WIDTH), kv.reshape(bsz, seq, 2 * KV_WIDTH),
                    g_a.reshape(bsz, seq, ATTN_WIDTH))
        x2 = _outproj(y_a.reshape(rows, RWKV_WIDTH), y_b.reshape(rows, ATTN_WIDTH), x2, w_out_l,
                      final_gain[None, :], final=(l == depth - 1))
    return x2.reshape(bsz, seq, D_MODEL)
```

```python
import numpy as np
import jax
import jax.numpy as jnp
from jax import lax
from jax.experimental import pallas as pl
from jax.experimental.pallas import tpu as pltpu

D_MODEL = 1024
HEAD_DIM = 64
RWKV_WIDTH = 512
ATTN_WIDTH = 512
KV_WIDTH = 128
LORA = 64
SHIFT_WIDTH = 3 * RWKV_WIDTH + 2 * LORA
IN_WIDTH = SHIFT_WIDTH + RWKV_WIDTH + ATTN_WIDTH + 2 * KV_WIDTH + ATTN_WIDTH
NORM_EPS = 1e-5
GN_EPS = 64e-5
WINDOW = 128
CHUNK = 128
LANES = 128
PAIRS = RWKV_WIDTH // LANES
CHUNKS_PER_STEP = 2
STEP_ROWS = CHUNKS_PER_STEP * CHUNK
ATTN_Q_BLOCKS = 4
ATTN_ROWS = ATTN_Q_BLOCKS * WINDOW
LOG2E = 1.4426950408889634
ROW_TILE = 512
SUB_ROWS = 256
VMEM_LIMIT = 48 * 1024 * 1024

F32 = jnp.float32
BF16 = jnp.bfloat16


def _mm(a, b):
    return jnp.dot(a.astype(BF16), b.astype(BF16), preferred_element_type=F32)


def _sigmoid(x):
    return 1.0 / (1.0 + jnp.exp(-x))


PROJ_WIDTHS = (SHIFT_WIDTH, RWKV_WIDTH, ATTN_WIDTH, 2 * KV_WIDTH, ATTN_WIDTH)


def _sub_rows(tm):
    return [slice(r0, r0 + SUB_ROWS) for r0 in range(0, tm, SUB_ROWS)]


def _norm_project(xs, gain_ref, w_ref, out_refs):
    hs = []
    for x in xs:
        ms = jnp.mean(x * x, axis=-1, keepdims=True)
        hs.append((x * lax.rsqrt(ms + NORM_EPS) * gain_ref[...]).astype(BF16))
    for h, rows in zip(hs, _sub_rows(out_refs[0].shape[0])):
        c0 = 0
        for ref in out_refs:
            w = ref.shape[-1]
            ref[rows, :] = jnp.dot(h, w_ref[:, c0:c0 + w], preferred_element_type=F32)
            c0 += w


def _inproj_kernel(x_ref, gain_ref, w_ref, *out_refs):
    _norm_project([x_ref[rows, :] for rows in _sub_rows(x_ref.shape[0])], gain_ref, w_ref, out_refs)


def _resident(shape):
    return pl.BlockSpec(shape, lambda i: (0,) * len(shape), pipeline_mode=pl.Buffered(1))


def _inproj(x2, gain, w_bf16):
    rows = x2.shape[0]
    tm = ROW_TILE
    return pl.pallas_call(
        _inproj_kernel,
        grid=(rows // tm,),
        in_specs=[
            pl.BlockSpec((tm, D_MODEL), lambda i: (i, 0)),
            _resident((1, D_MODEL)),
            _resident((D_MODEL, IN_WIDTH)),
        ],
        out_specs=[pl.BlockSpec((tm, w), lambda i: (i, 0)) for w in PROJ_WIDTHS],
        out_shape=[jax.ShapeDtypeStruct((rows, w), F32) for w in PROJ_WIDTHS],
        compiler_params=pltpu.CompilerParams(
            dimension_semantics=("arbitrary",), vmem_limit_bytes=VMEM_LIMIT),
        name="inproj",
    )(x2, gain, w_bf16)


def _out_residual(ya_ref, yb_ref, x_ref, w_ref, rows):
    y = (jnp.dot(ya_ref[rows, :], w_ref[:RWKV_WIDTH, :], preferred_element_type=F32)
         + jnp.dot(yb_ref[rows, :], w_ref[RWKV_WIDTH:, :], preferred_element_type=F32))
    return x_ref[rows, :] + y


def _mid_kernel(ya_ref, yb_ref, x_ref, wo_ref, gain_ref, wi_ref, xo_ref, *out_refs):
    xs = []
    for rows in _sub_rows(x_ref.shape[0]):
        xn = _out_residual(ya_ref, yb_ref, x_ref, wo_ref, rows)
        xo_ref[rows, :] = xn
        xs.append(xn)
    _norm_project(xs, gain_ref, wi_ref, out_refs)


def _mid(ya, yb, x2, wo_bf16, gain, wi_bf16):
    rows = x2.shape[0]
    tm = ROW_TILE
    row_spec = lambda w: pl.BlockSpec((tm, w), lambda i: (i, 0))
    return pl.pallas_call(
        _mid_kernel,
        grid=(rows // tm,),
        in_specs=[row_spec(RWKV_WIDTH), row_spec(ATTN_WIDTH), row_spec(D_MODEL),
                  _resident((D_MODEL, D_MODEL)), _resident((1, D_MODEL)), _resident((D_MODEL, IN_WIDTH))],
        out_specs=[row_spec(D_MODEL)] + [row_spec(w) for w in PROJ_WIDTHS],
        out_shape=[jax.ShapeDtypeStruct((rows, D_MODEL), F32)]
        + [jax.ShapeDtypeStruct((rows, w), F32) for w in PROJ_WIDTHS],
        compiler_params=pltpu.CompilerParams(
            dimension_semantics=("arbitrary",), vmem_limit_bytes=VMEM_LIMIT),
        name="outproj_inproj",
    )(ya, yb, x2, wo_bf16, gain, wi_bf16)


def _head_sum(x, ones_pair):
    return jnp.concatenate(
        [_mm(x[:, pr * LANES:(pr + 1) * LANES], ones_pair) for pr in range(PAIRS)], axis=1)


def _swap_pairs(x):
    return jnp.concatenate(
        [pltpu.roll(x[:, pr * LANES:(pr + 1) * LANES], HEAD_DIM, 1) for pr in range(PAIRS)], axis=1)


def _chunk_scratch():
    return [
        pltpu.VMEM((2 * CHUNK, RWKV_WIDTH), BF16),
        pltpu.VMEM((RWKV_WIDTH, 2 * CHUNK), BF16),
        pltpu.VMEM((2 * RWKV_WIDTH, CHUNK), BF16),
        pltpu.VMEM((CHUNK, RWKV_WIDTH), BF16),
        pltpu.VMEM((2, CHUNK, RWKV_WIDTH), BF16),
        pltpu.VMEM((CHUNK, RWKV_WIDTH), F32),
        pltpu.VMEM((CHUNK, RWKV_WIDTH), F32),
        pltpu.VMEM((CHUNK, RWKV_WIDTH), F32),
        pltpu.VMEM((8, RWKV_WIDTH), F32),
    ]


N_CHUNK_SCRATCH = len(_chunk_scratch())


def _frontend(ps_ref, ck, prev_last, params, ones_pair, tri, scr):
    mu_ref, w0_ref, wup_ref, a0_ref, aup_ref, kkw_ref, ka_ref, rk_ref = params
    lhs_ref, rhs_ref, ht_ref, vsw_ref, vso_ref, as_ref, rs_ref, bonus_ref, gam_ref = scr

    p = ps_ref[0, ck * CHUNK:(ck + 1) * CHUNK, :]
    rows1 = lax.broadcasted_iota(jnp.int32, (CHUNK, 1), 0)
    prev =jnp.where(rows1 == 0, prev_last, pltpu.roll(p, 1, 0))
    sh = p + (prev - p) * mu_ref[...]
    r = sh[:, 0:RWKV_WIDTH]
    k = sh[:, RWKV_WIDTH:2 * RWKV_WIDTH]
    v = sh[:, 2 * RWKV_WIDTH:3 * RWKV_WIDTH]
    wa = sh[:, 3 * RWKV_WIDTH:]
    lw = _mm(jnp.tanh(wa), wup_ref[...])
    la = _mm(wa, aup_ref[...])
    kkraw = k * kkw_ref[...]
    n2 = _head_sum(kkraw * kkraw, ones_pair)
    yield

    ld = _sigmoid(w0_ref[...] + lw) * float(-LOG2E * np.exp(-0.5))
    a = _sigmoid(a0_ref[...] + la)
    kk = kkraw * lax.rsqrt(jnp.maximum(n2, 1e-24))
    kp = k * (1.0 + (a - 1.0) * ka_ref[...])
    bv = kk * a
    bonus_ref[...] = _head_sum(r * kp * rk_ref[...], ones_pair) * v
    l_hi = ld.astype(BF16)
    rem = ld - l_hi.astype(F32)
    l_mid = rem.astype(BF16)
    l_lo = (rem - l_mid.astype(F32)).astype(BF16)
    cs = (jnp.dot(tri, l_hi, preferred_element_type=F32)
          + jnp.dot(tri, l_mid, preferred_element_type=F32)
          + jnp.dot(tri, l_lo, preferred_element_type=F32))
    yield

    mid = cs[CHUNK // 2 - 1:CHUNK // 2, :]
    end = cs[CHUNK - 1:CHUNK, :]
    e_out = jnp.exp2(mid - cs)
    e_end = jnp.exp2(end - cs)
    a_t = -kk * jnp.exp2(cs - ld - mid)
    r_t = r * jnp.exp2(cs - mid)
    g_mid = jnp.exp2(mid)
    as_ref[...] = a_t * g_mid
    rs_ref[...] = r_t * g_mid
    gam_ref[...] = jnp.broadcast_to(jnp.exp2(end), gam_ref.shape)
    lo = (lax.broadcasted_iota(jnp.int32, (CHUNK, RWKV_WIDTH), 1) & HEAD_DIM) == 0
    lhs_ref[:CHUNK, :] = a_t.astype(BF16)
    lhs_ref[CHUNK:, :] = r_t.astype(BF16)
    rhs_ref[:, :CHUNK] = (bv * e_out).T.astype(BF16)
    rhs_ref[:, CHUNK:] = (kp * e_out).T.astype(BF16)
    ht_ref[:RWKV_WIDTH, :] = (bv * e_end).T.astype(BF16)
    ht_ref[RWKV_WIDTH:, :] = (kp * e_end).T.astype(BF16)
    v_sw = _swap_pairs(v)
    vsw_ref[...] = v_sw.astype(BF16)
    vso_ref[0] = jnp.where(lo, 0.0, v_sw).astype(BF16)
    vso_ref[1] = jnp.where(lo, v_sw, 0.0).astype(BF16)
    yield


def _solver(g_ref, gnw_ref, gnb_ref, ones_pair, out_ref, st_ref, scrs):
    row = lax.broadcasted_iota(jnp.int32, (CHUNK, CHUNK), 0)
    col = lax.broadcasted_iota(jnp.int32, (CHUNK, CHUNK), 1)
    strict = col < row
    incl2 = jnp.concatenate([col <= row, col <= row], axis=1)
    lo = col < HEAD_DIM
    owns = [lo, jnp.logical_not(lo)]
    same_head = (row >> 6) == (col >> 6)
    eye = row == col
    blk8 = (row >> 3) == (col >> 3)
    n_ck = len(scrs)
    heads = [(ck, pr, hh) for ck in range(n_ck) for pr in range(PAIRS) for hh in range(2)]
    sls = [slice(pr * LANES, (pr + 1) * LANES) for pr in range(PAIRS)]
    lhs_refs, rhs_refs, ht_refs, vsw_refs, vso_refs, as_refs, rs_refs, bonus_refs, gam_refs = zip(*scrs)

    zrows = jnp.zeros((HEAD_DIM, 2 * CHUNK), BF16)

    def head_rows(ref, pr, hh):
        rows = ref[pr * LANES + hh * HEAD_DIM:pr * LANES + (hh + 1) * HEAD_DIM, :]
        return jnp.concatenate([rows, zrows] if hh == 0 else [zrows, rows], axis=0)

    amats = [jnp.dot(lhs_refs[ck][:, sls[pr]], head_rows(rhs_refs[ck], pr, hh), preferred_element_type=F32)
             for ck, pr, hh in heads]
    yield
    n_mats = [jnp.where(strict, am[:CHUNK, :CHUNK], 0.0) for am in amats]
    avs = [jnp.dot(jnp.where(strict, am[:CHUNK, CHUNK:], 0.0).astype(BF16), vsw_refs[ck][:, sls[pr]],
                   preferred_element_type=F32) for (ck, pr, hh), am in zip(heads, amats)]
    nds = [jnp.where(blk8, n, 0.0) for n in n_mats]
    ndbs = [nd.astype(BF16) for nd in nds]
    s2s = [jnp.dot(ndb, ndb, preferred_element_type=F32) for ndb in ndbs]
    yield
    ps = [jnp.where(eye, 1.0, nd) for nd in nds]
    pss = [_mm(jnp.concatenate([p, s2], axis=0), s2) for p, s2 in zip(ps, s2s)]
    yield
    ps = [p + x[:CHUNK] for p, x in zip(ps, pss)]
    dbs = [(p + _mm(p, x[CHUNK:])).astype(BF16) for p, x in zip(ps, pss)]
    yield
    for sh in (3, 4, 5):
        off = ((row >> (sh + 1)) == (col >> (sh + 1))) & ((row >> sh) != (col >> sh))
        ts = [jnp.dot(jnp.where(off, n, 0.0).astype(BF16), db, preferred_element_type=F32)
              for n, db in zip(n_mats, dbs)]
        yield
        dbs = [db + jnp.dot(db, t.astype(BF16), preferred_element_type=F32).astype(BF16) for db, t in zip(dbs, ts)]
        yield
    half = CHUNK // 2
    y0s = [jnp.dot(db, jnp.where(owns[hh], as_refs[ck][:, sls[pr]], av).astype(BF16), preferred_element_type=F32)
           for (ck, pr, hh), db, av in zip(heads, dbs, avs)]
    yield
    y0bs = [y0.astype(BF16) for y0 in y0s]
    lo_half = lax.broadcasted_iota(jnp.int32, (half, CHUNK), 1) < half
    ws = [jnp.dot(jnp.where(lo_half, n[half:], 0.0).astype(BF16), y0b, preferred_element_type=F32)
          for n, y0b in zip(n_mats, y0bs)]
    yield
    xss = [jnp.concatenate(
        [y0b[:half],
         (y0[half:] + jnp.dot(db[half:], jnp.concatenate([y0b[:half], w.astype(BF16)], axis=0),
                              preferred_element_type=F32)).astype(BF16)], axis=0)
        for y0, y0b, db, w in zip(y0s, y0bs, dbs, ws)]
    yield
    zs = [_mm(jnp.where(incl2, am[CHUNK:, :], 0.0),
              jnp.concatenate([xs, vso_refs[ck][hh, :, sls[pr]]], axis=0))
          for (ck, pr, hh), am, xs in zip(heads, amats, xss)]
    res = [jnp.dot(ht_refs[ck][sls[pr], :], xs, preferred_element_type=F32)
           for (ck, pr, hh), xs in zip(heads, xss)]
    kvs = [[jnp.dot(ht_refs[ck][RWKV_WIDTH + pr * LANES:RWKV_WIDTH + (pr + 1) * LANES, :],
                    vsw_refs[ck][:, sls[pr]], preferred_element_type=F32) for pr in range(PAIRS)]
           for ck in range(n_ck)]
    yield
    ys = [[] for _ in range(n_ck)]
    for pr in range(PAIRS):
        sl = sls[pr]
        st = st_ref[pr]
        for ck in range(n_ck):
            i0 = (ck * PAIRS + pr) * 2
            z0, z1 = zs[i0], zs[i0 + 1]
            cat = jnp.concatenate([res[i0][:HEAD_DIM], res[i0 + 1][HEAD_DIM:]], axis=0)
            m_t = jnp.where(same_head, cat, 0.0) + jnp.where(eye, gam_refs[ck][0:1, sl], 0.0)
            g_t = jnp.where(same_head, 0.0, cat + kvs[ck][pr])
            q_hat = rs_refs[ck][:, sl] + jnp.where(lo, z0, z1)
            y_loc_sw = jnp.where(lo, z1, z0)
            stb = st.astype(BF16)
            ys[ck].append(_mm(q_hat, stb) + y_loc_sw)
            st = _mm(m_t, stb) + g_t
        st_ref[pr] = st
    yield
    for ck in range(n_ck):
        rows = slice(ck * CHUNK, (ck + 1) * CHUNK)
        y = _swap_pairs(jnp.concatenate(ys[ck], axis=1))
        mean = _head_sum(y, ones_pair) * (1.0 / HEAD_DIM)
        dlt = y - mean
        var = _head_sum(dlt * dlt, ones_pair) * (1.0 / HEAD_DIM)
        yn = dlt * lax.rsqrt(var + GN_EPS) * gnw_ref[...] + gnb_ref[...]
        g = g_ref[0, rows, :]
        out_ref[0, rows, :] = ((yn + bonus_refs[ck][...]) * (g * _sigmoid(g))).astype(out_ref.dtype)
    yield


_STAGE_ORDER = "FSSFSSS"


def _rwkv_kernel(ps_ref, prev_ref, g_ref, mu_ref, w0_ref, wup_ref, a0_ref, aup_ref,
                 kkw_ref, ka_ref, rk_ref, gnw_ref, gnb_ref, ones_ref, tri_ref,
                 out_ref, st_ref, *chunk_scr):
    s = pl.program_id(1)
    sets = [chunk_scr[i * N_CHUNK_SCRATCH:(i + 1) * N_CHUNK_SCRATCH] for i in range(2 * CHUNKS_PER_STEP)]
    slot_a, slot_b = sets[:CHUNKS_PER_STEP], sets[CHUNKS_PER_STEP:]

    @pl.when(s == 0)
    def _():
        st_ref[...] = jnp.zeros_like(st_ref)
        for scr in slot_b:
            for ref in scr:
                ref[...] = jnp.zeros_like(ref)

    params = (mu_ref, w0_ref, wup_ref, a0_ref, aup_ref, kkw_ref, ka_ref, rk_ref)

    def step(fe_scrs, sv_scrs):
        fes = []
        for ck in range(CHUNKS_PER_STEP):
            if ck == 0:
                prev_last = prev_ref[0, 7:8, :] * jnp.where(s > 0, 1.0, 0.0)
            else:
                prev_last = ps_ref[0, ck * CHUNK - 1:ck * CHUNK, :]
            fes.append(_frontend(ps_ref, ck, prev_last, params, ones_ref[...], tri_ref[...], fe_scrs[ck]))
        sv = _solver(g_ref, gnw_ref, gnb_ref, ones_ref[...], out_ref, st_ref, sv_scrs)
        for key in _STAGE_ORDER:
            for gen in (fes if key == "F" else [sv]):
                next(gen)
        for gen in [sv] + fes:
            for _ in gen:
                pass

    @pl.when((s & 1) == 0)
    def _():
        step(slot_a, slot_b)

    @pl.when((s & 1) == 1)
    def _():
        step(slot_b, slot_a)


def _rwkv(ps, g_r, mu, w0, wup_pad, a0, aup_pad, kkw, ka, rk, gnw, gnb, ones_pair, tri):
    bsz, seq, _ = ps.shape
    ns = seq // STEP_ROWS
    sub = STEP_ROWS // 8

    def vec(width):
        return pl.BlockSpec((1, width), lambda b, s: (0, 0))

    def fe_block(s):
        return jnp.minimum(s, ns - 1)

    def sv_block(s):
        return jnp.maximum(s - 1, 0)

    return pl.pallas_call(
        _rwkv_kernel,
        grid=(bsz, ns + 1),
        in_specs=[
            pl.BlockSpec((1, STEP_ROWS, SHIFT_WIDTH), lambda b, s: (b, fe_block(s), 0)),
            pl.BlockSpec((1, 8, SHIFT_WIDTH), lambda b, s: (b, jnp.maximum(fe_block(s) * sub - 1, 0), 0)),
            pl.BlockSpec((1, STEP_ROWS, RWKV_WIDTH), lambda b, s: (b, sv_block(s), 0)),
            vec(SHIFT_WIDTH),
            vec(RWKV_WIDTH),
            pl.BlockSpec((LANES, RWKV_WIDTH), lambda b, s: (0, 0)),
            vec(RWKV_WIDTH),
            pl.BlockSpec((LANES, RWKV_WIDTH), lambda b, s: (0, 0)),
            vec(RWKV_WIDTH), vec(RWKV_WIDTH), vec(RWKV_WIDTH), vec(RWKV_WIDTH), vec(RWKV_WIDTH),
            pl.BlockSpec((LANES, LANES), lambda b, s: (0, 0)),
            pl.BlockSpec((CHUNK, CHUNK), lambda b, s: (0, 0)),
        ],
        out_specs=pl.BlockSpec((1, STEP_ROWS, RWKV_WIDTH), lambda b, s: (b, sv_block(s), 0)),
        out_shape=jax.ShapeDtypeStruct((bsz, seq, RWKV_WIDTH), BF16),
        scratch_shapes=[pltpu.VMEM((PAIRS, LANES, LANES), F32)]
        + [buf for _ in range(2 * CHUNKS_PER_STEP) for buf in _chunk_scratch()],
        compiler_params=pltpu.CompilerParams(
            dimension_semantics=("arbitrary", "arbitrary"), vmem_limit_bytes=VMEM_LIMIT),
        name="rwkv7_chunked",
    )(ps, ps, g_r, mu, w0, wup_pad, a0, aup_pad, kkw, ka, rk, gnw, gnb, ones_pair, tri)


def _attn_kernel(sinks_ref, q_ref, kvc_ref, kvp_ref, g_ref, o_ref):
    n = pl.program_id(1)
    kv = jnp.concatenate([kvp_ref[0], kvc_ref[0]], axis=0)
    k_nat = kv[:, :KV_WIDTH]
    v_nat = kv[:, KV_WIDTH:]
    ks = [k_nat.astype(BF16), pltpu.roll(k_nat, HEAD_DIM, 1).astype(BF16)]
    vts = [v_nat.T.astype(BF16), pltpu.roll(v_nat, HEAD_DIM, 1).T.astype(BF16)]
    si = lax.broadcasted_iota(jnp.int32, (2 * WINDOW, WINDOW), 0)
    qi = lax.broadcasted_iota(jnp.int32, (2 * WINDOW, WINDOW), 1)
    band = (si > qi) & (si <= qi + WINDOW)
    first_valid = jnp.where(n > 0, 0, WINDOW)
    lo = lax.broadcasted_iota(jnp.int32, (WINDOW, LANES), 1) < HEAD_DIM
    owns = [lo, jnp.logical_not(lo)]
    qs = q_ref[0] * (HEAD_DIM ** -0.5 * LOG2E)
    n_blk = ATTN_WIDTH // LANES
    group = n_blk // 2

    def scores(j):
        rows = slice(j * WINDOW, (j + 1) * WINDOW)
        keys = slice(j * WINDOW, (j + 2) * WINDOW)
        out = []
        for blk in range(n_blk):
            qb = qs[rows, blk * LANES:(blk + 1) * LANES]
            for hf in range(2):
                kk = ks[0 if blk // group == hf else 1][keys]
                out.append(lax.dot_general(kk, jnp.where(owns[hf], qb, 0.0).astype(BF16),
                                           (((1,), (1,)), ((), ())), preferred_element_type=F32))
        return out

    def finish(j, sc):
        rows = slice(j * WINDOW, (j + 1) * WINDOW)
        keys = slice(j * WINDOW, (j + 2) * WINDOW)
        mask = band & (si >= first_valid) if j == 0 else band
        outs = []
        for blk in range(n_blk):
            halves = []
            for hf in range(2):
                s = jnp.where(mask, sc[2 * blk + hf], -1e30)
                sink = sinks_ref[2 * blk + hf] * LOG2E
                m = jnp.maximum(jnp.max(s, axis=0, keepdims=True), sink)
                p = jnp.exp2(s - m)
                denom = jnp.sum(p, axis=0, keepdims=True) + jnp.exp2(sink - m)
                vt = vts[0 if blk // group == hf else 1][:, keys]
                halves.append(jnp.dot(vt, p.astype(BF16), preferred_element_type=F32) / denom)
            outs.append(jnp.concatenate([halves[0][:HEAD_DIM], halves[1][HEAD_DIM:]], axis=0).T)
        g = g_ref[0, rows, :]
        o_ref[0, rows, :] = (jnp.concatenate(outs, axis=1) * (g * _sigmoid(g))).astype(o_ref.dtype)

    sc = scores(0)
    for j in range(ATTN_Q_BLOCKS):
        nxt = scores(j + 1) if j + 1 < ATTN_Q_BLOCKS else None
        finish(j, sc)
        sc = nxt


def _attn(sinks, q, kv, g_a):
    bsz, seq, _ = q.shape
    nb = seq // ATTN_ROWS
    return pl.pallas_call(
        _attn_kernel,
        grid=(bsz, nb),
        in_specs=[
            pl.BlockSpec(memory_space=pltpu.SMEM),
            pl.BlockSpec((1, ATTN_ROWS, ATTN_WIDTH), lambda b, n: (b, n, 0)),
            pl.BlockSpec((1, ATTN_ROWS, 2 * KV_WIDTH), lambda b, n: (b, n, 0)),
            pl.BlockSpec((1, WINDOW, 2 * KV_WIDTH), lambda b, n: (b, jnp.maximum(n * ATTN_Q_BLOCKS - 1, 0), 0)),
            pl.BlockSpec((1, ATTN_ROWS, ATTN_WIDTH), lambda b, n: (b, n, 0)),
        ],
        out_specs=pl.BlockSpec((1, ATTN_ROWS, ATTN_WIDTH), lambda b, n: (b, n, 0)),
        out_shape=jax.ShapeDtypeStruct((bsz, seq, ATTN_WIDTH), BF16),
        compiler_params=pltpu.CompilerParams(
            dimension_semantics=("arbitrary", "arbitrary"), vmem_limit_bytes=VMEM_LIMIT),
        name="swa_sink",
    )(sinks, q, kv, kv, g_a)


def _final_kernel(ya_ref, yb_ref, x_ref, w_ref, fg_ref, o_ref):
    for rows in _sub_rows(x_ref.shape[0]):
        xn = _out_residual(ya_ref, yb_ref, x_ref, w_ref, rows)
        ms = jnp.mean(xn * xn, axis=-1, keepdims=True)
        o_ref[rows, :] = xn * lax.rsqrt(ms + NORM_EPS) * fg_ref[...]


def _final(ya, yb, x2, w_bf16, fgain):
    rows = x2.shape[0]
    tm = ROW_TILE
    row_spec = lambda w: pl.BlockSpec((tm, w), lambda i: (i, 0))
    return pl.pallas_call(
        _final_kernel,
        grid=(rows // tm,),
        in_specs=[row_spec(RWKV_WIDTH), row_spec(ATTN_WIDTH), row_spec(D_MODEL),
                  _resident((D_MODEL, D_MODEL)), _resident((1, D_MODEL))],
        out_specs=row_spec(D_MODEL),
        out_shape=jax.ShapeDtypeStruct((rows, D_MODEL), F32),
        compiler_params=pltpu.CompilerParams(
            dimension_semantics=("arbitrary",), vmem_limit_bytes=VMEM_LIMIT),
        name="outproj_final",
    )(ya, yb, x2, w_bf16, fgain)


def kernel(x, norm_gain, w_in, shift_mu, w0, w_up, a0, a_up, k_k, k_a, r_k, gn_w, gn_b, sinks, w_out, final_gain):
    bsz, seq, _ = x.shape
    depth = w_in.shape[0]
    rows = bsz * seq

    head_id = np.arange(LANES) // HEAD_DIM
    ones_pair = jnp.asarray(head_id[:, None] == head_id[None, :], dtype=BF16)
    tri = jnp.asarray(np.tril(np.ones((CHUNK, CHUNK), np.float32)), dtype=BF16)
    zpad = jnp.zeros((LORA, RWKV_WIDTH), F32)
    w_in_b = w_in.astype(BF16)
    w_out_b = w_out.astype(BF16)

    x2 = x.reshape(rows, D_MODEL)
    proj = _inproj(x2, norm_gain[0][None, :], w_in_b[0])
    for l in range(depth):
        ps, g_r, q, kv, g_a = proj
        wup_pad = jnp.concatenate([w_up[l], zpad], axis=0).astype(BF16)
        aup_pad = jnp.concatenate([zpad, a_up[l]], axis=0).astype(BF16)
        y_a = _rwkv(ps.reshape(bsz, seq, SHIFT_WIDTH), g_r.reshape(bsz, seq, RWKV_WIDTH),
                    shift_mu[l][None, :], w0[l][None, :], wup_pad, a0[l][None, :], aup_pad,
                    k_k[l][None, :], k_a[l][None, :], r_k[l].reshape(1, RWKV_WIDTH),
                    gn_w[l][None, :], gn_b[l][None, :], ones_pair, tri)
        y_b = _attn(sinks[l], q.reshape(bsz, seq, ATTN_WIDTH), kv.reshape(bsz, seq, 2 * KV_WIDTH),
                    g_a.reshape(bsz, seq, ATTN_WIDTH))
        y_a = y_a.reshape(rows, RWKV_WIDTH)
        y_b = y_b.reshape(rows, ATTN_WIDTH)
        if l + 1 < depth:
            x2, *proj = _mid(y_a, y_b, x2, w_out_b[l], norm_gain[l + 1][None, :], w_in_b[l + 1])
        else:
            x2 = _final(y_a, y_b, x2, w_out_b[l], final_gain[None, :])
    return x2.reshape(bsz, seq, D_MODEL)
```

```python
import functools

import numpy as np
import jax
import jax.numpy as jnp
from jax import lax
from jax.experimental import pallas as pl
from jax.experimental.pallas import tpu as pltpu

D_MODEL = 1024
HEAD_DIM = 64
RWKV_WIDTH = 512
ATTN_WIDTH = 512
KV_WIDTH = 128
LORA = 64
SHIFT_WIDTH = 3 * RWKV_WIDTH + 2 * LORA
IN_WIDTH = SHIFT_WIDTH + RWKV_WIDTH + ATTN_WIDTH + 2 * KV_WIDTH + ATTN_WIDTH
NORM_EPS = 1e-5
GN_EPS = 64e-5
WINDOW = 128
CHUNK = 128
LANES = 128
PAIRS = RWKV_WIDTH // LANES
CHUNKS_PER_STEP = 2
STEP_ROWS = CHUNKS_PER_STEP * CHUNK
ATTN_Q_BLOCKS = 4
ATTN_ROWS = ATTN_Q_BLOCKS * WINDOW
LOG2E = 1.4426950408889634
ROW_TILE = 512
SUB_ROWS = 256
VMEM_LIMIT = 48 * 1024 * 1024

F32 = jnp.float32
BF16 = jnp.bfloat16


def _mm(a, b):
    return jnp.dot(a.astype(BF16), b.astype(BF16), preferred_element_type=F32)


def _sigmoid(x):
    return 1.0 / (1.0 + jnp.exp(-x))


PROJ_WIDTHS = (SHIFT_WIDTH, RWKV_WIDTH, ATTN_WIDTH, 2 * KV_WIDTH, ATTN_WIDTH)


def _sub_rows(tm):
    return [slice(r0, r0 + SUB_ROWS) for r0 in range(0, tm, SUB_ROWS)]


def _norm_project(xs, gain_ref, w_ref, out_refs):
    hs = []
    for x in xs:
        ms = jnp.mean(x * x, axis=-1, keepdims=True)
        hs.append((x * lax.rsqrt(ms + NORM_EPS) * gain_ref[...]).astype(BF16))
    for h, rows in zip(hs, _sub_rows(out_refs[0].shape[0])):
        c0 = 0
        for ref in out_refs:
            w = ref.shape[-1]
            ref[rows, :] = jnp.dot(h, w_ref[:, c0:c0 + w], preferred_element_type=F32)
            c0 += w


def _inproj_kernel(x_ref, gain_ref, w_ref, *out_refs):
    _norm_project([x_ref[rows, :] for rows in _sub_rows(x_ref.shape[0])], gain_ref, w_ref, out_refs)


def _resident(shape):
    return pl.BlockSpec(shape, lambda i: (0,) * len(shape), pipeline_mode=pl.Buffered(1))


def _inproj(x2, gain, w_bf16):
    rows = x2.shape[0]
    tm = ROW_TILE
    return pl.pallas_call(
        _inproj_kernel,
        grid=(rows // tm,),
        in_specs=[
            pl.BlockSpec((tm, D_MODEL), lambda i: (i, 0)),
            _resident((1, D_MODEL)),
            _resident((D_MODEL, IN_WIDTH)),
        ],
        out_specs=[pl.BlockSpec((tm, w), lambda i: (i, 0)) for w in PROJ_WIDTHS],
        out_shape=[jax.ShapeDtypeStruct((rows, w), F32) for w in PROJ_WIDTHS],
        compiler_params=pltpu.CompilerParams(
            dimension_semantics=("arbitrary",), vmem_limit_bytes=VMEM_LIMIT),
        name="inproj",
    )(x2, gain, w_bf16)


def _out_residual(ya_ref, yb_ref, x_ref, w_ref, rows):
    y = (jnp.dot(ya_ref[rows, :], w_ref[:RWKV_WIDTH, :], preferred_element_type=F32)
         + jnp.dot(yb_ref[rows, :], w_ref[RWKV_WIDTH:, :], preferred_element_type=F32))
    return x_ref[rows, :] + y


def _mid_kernel(ya_ref, yb_ref, x_ref, wo_ref, gain_ref, wi_ref, xo_ref, *out_refs):
    xs = []
    for rows in _sub_rows(x_ref.shape[0]):
        xn = _out_residual(ya_ref, yb_ref, x_ref, wo_ref, rows)
        xo_ref[rows, :] = xn
        xs.append(xn)
    _norm_project(xs, gain_ref, wi_ref, out_refs)


def _mid(ya, yb, x2, wo_bf16, gain, wi_bf16):
    rows = x2.shape[0]
    tm = ROW_TILE
    row_spec = lambda w: pl.BlockSpec((tm, w), lambda i: (i, 0))
    return pl.pallas_call(
        _mid_kernel,
        grid=(rows // tm,),
        in_specs=[row_spec(RWKV_WIDTH), row_spec(ATTN_WIDTH), row_spec(D_MODEL),
                  _resident((D_MODEL, D_MODEL)), _resident((1, D_MODEL)), _resident((D_MODEL, IN_WIDTH))],
        out_specs=[row_spec(D_MODEL)] + [row_spec(w) for w in PROJ_WIDTHS],
        out_shape=[jax.ShapeDtypeStruct((rows, D_MODEL), F32)]
        + [jax.ShapeDtypeStruct((rows, w), F32) for w in PROJ_WIDTHS],
        compiler_params=pltpu.CompilerParams(
            dimension_semantics=("arbitrary",), vmem_limit_bytes=VMEM_LIMIT),
        name="outproj_inproj",
    )(ya, yb, x2, wo_bf16, gain, wi_bf16)


def _head_sum(x, ones_pair):
    return jnp.concatenate(
        [_mm(x[:, pr * LANES:(pr + 1) * LANES], ones_pair) for pr in range(PAIRS)], axis=1)


def _swap_pairs(x):
    return jnp.concatenate(
        [pltpu.roll(x[:, pr * LANES:(pr + 1) * LANES], HEAD_DIM, 1) for pr in range(PAIRS)], axis=1)


def _chunk_scratch():
    return [
        pltpu.VMEM((2 * CHUNK, RWKV_WIDTH), BF16),
        pltpu.VMEM((RWKV_WIDTH, 2 * CHUNK), BF16),
        pltpu.VMEM((2 * RWKV_WIDTH, CHUNK), BF16),
        pltpu.VMEM((CHUNK, RWKV_WIDTH), BF16),
        pltpu.VMEM((2, CHUNK, RWKV_WIDTH), BF16),
        pltpu.VMEM((CHUNK, RWKV_WIDTH), F32),
        pltpu.VMEM((CHUNK, RWKV_WIDTH), F32),
        pltpu.VMEM((CHUNK, RWKV_WIDTH), F32),
        pltpu.VMEM((8, RWKV_WIDTH), F32),
    ]


N_CHUNK_SCRATCH = len(_chunk_scratch())


def _frontend(ps_ref, ck, prev_last, params, ones_pair, tri, scr, anchors):
    mu_ref, w0_ref, wup_ref, a0_ref, aup_ref, kkw_ref, ka_ref, rk_ref = params
    lhs_ref, rhs_ref, ht_ref, vsw_ref, vso_ref, as_ref, rs_ref, bonus_ref, gam_ref = scr
    rows = slice(ck * CHUNK, (ck + 1) * CHUNK)
    first_row = lax.broadcasted_iota(jnp.int32, (CHUNK, 1), 0) == 0
    lo = lax.broadcasted_iota(jnp.int32, (CHUNK, LANES), 1) < HEAD_DIM

    def shifted(cols, anchor):
        p = ps_ref[0, rows, cols]
        prev = jnp.where(first_row, prev_last[:, cols], pltpu.roll(p, 1, 0))
        return p + (prev - p) * (mu_ref[:, cols] + anchor)

    wa = shifted(slice(3 * RWKV_WIDTH, SHIFT_WIDTH), 0.0)
    wa_b = wa.astype(BF16)
    wt_b = jnp.tanh(wa).astype(BF16)
    yield
    lw = jnp.dot(wt_b, wup_ref[...], preferred_element_type=F32)
    la = jnp.dot(wa_b, aup_ref[...], preferred_element_type=F32)
    bonus_pending = None
    for pr in range(PAIRS):
        sl = slice(pr * LANES, (pr + 1) * LANES)
        anchor = anchors[-1]
        r = shifted(slice(pr * LANES, (pr + 1) * LANES), anchor)
        k = shifted(slice(RWKV_WIDTH + pr * LANES, RWKV_WIDTH + (pr + 1) * LANES), anchor)
        v = shifted(slice(2 * RWKV_WIDTH + pr * LANES, 2 * RWKV_WIDTH + (pr + 1) * LANES), anchor)
        kkraw = k * kkw_ref[:, sl]
        sq_b = (kkraw * kkraw).astype(BF16)
        ld = _sigmoid(w0_ref[:, sl] + lw[:, sl]) * float(-LOG2E * np.exp(-0.5))
        a = _sigmoid(a0_ref[:, sl] + la[:, sl])
        kp = k * (1.0 + (a - 1.0) * ka_ref[:, sl])
        rkp_b = (r * kp * rk_ref[:, sl]).astype(BF16)
        l_hi = ld.astype(BF16)
        rem = ld - l_hi.astype(F32)
        l_mid = rem.astype(BF16)
        l_lo = (rem - l_mid.astype(F32)).astype(BF16)
        yield
        if bonus_pending is not None:
            bonus_pending()
        n2 = jnp.dot(sq_b, ones_pair, preferred_element_type=F32)
        cs = (jnp.dot(tri, l_hi, preferred_element_type=F32)
              + jnp.dot(tri, l_mid, preferred_element_type=F32)
              + jnp.dot(tri, l_lo, preferred_element_type=F32))

        def bonus_pending(sl=sl, rkp_b=rkp_b, v=v):
            bonus_ref[:, sl] = jnp.dot(rkp_b, ones_pair, preferred_element_type=F32) * v

        yield
        kk = kkraw * lax.rsqrt(jnp.maximum(n2, 1e-24))
        bv = kk * a
        mid = cs[CHUNK // 2 - 1:CHUNK // 2, :]
        end = cs[CHUNK - 1:CHUNK, :]
        e_out = jnp.exp2(mid - cs)
        e_end = jnp.exp2(end - cs)
        a_t = -kk * jnp.exp2(cs - ld - mid)
        r_t = r * jnp.exp2(cs - mid)
        g_mid = jnp.exp2(mid)
        as_ref[:, sl] = a_t * g_mid
        rs_ref[:, sl] = r_t * g_mid
        gam_ref[:, sl] = jnp.broadcast_to(jnp.exp2(end), (gam_ref.shape[0], LANES))
        lhs_ref[:CHUNK, sl] = a_t.astype(BF16)
        lhs_ref[CHUNK:, sl] = r_t.astype(BF16)
        rhs_ref[sl, :CHUNK] = (bv * e_out).T.astype(BF16)
        rhs_ref[sl, CHUNK:] = (kp * e_out).T.astype(BF16)
        ht_ref[sl, :] = (bv * e_end).T.astype(BF16)
        ht_ref[RWKV_WIDTH + pr * LANES:RWKV_WIDTH + (pr + 1) * LANES, :] = (kp * e_end).T.astype(BF16)
        v_sw = pltpu.roll(v, HEAD_DIM, 1)
        vsw_ref[:, sl] = v_sw.astype(BF16)
        vso_ref[0, :, sl] = jnp.where(lo, 0.0, v_sw).astype(BF16)
        vso_ref[1, :, sl] = jnp.where(lo, v_sw, 0.0).astype(BF16)
        yield
    bonus_pending()
    yield


def _zero_row(x):
    row = x[0:1, :]
    return jnp.where(row != row, 1.0, 0.0)


def _solver(g_ref, gnw_ref, gnb_ref, ones_pair, out_ref, st_ref, scrs, anchors):
    row = lax.broadcasted_iota(jnp.int32, (CHUNK, CHUNK), 0)
    col = lax.broadcasted_iota(jnp.int32, (CHUNK, CHUNK), 1)
    strict = col < row
    incl2 = jnp.concatenate([col <= row, col <= row], axis=1)
    lo = col < HEAD_DIM
    owns = [lo, jnp.logical_not(lo)]
    same_head = (row >> 6) == (col >> 6)
    eye = row == col
    blk8 = (row >> 3) == (col >> 3)
    n_ck = len(scrs)
    heads = [(ck, pr, hh) for ck in range(n_ck) for pr in range(PAIRS) for hh in range(2)]
    sls = [slice(pr * LANES, (pr + 1) * LANES) for pr in range(PAIRS)]
    lhs_refs, rhs_refs, ht_refs, vsw_refs, vso_refs, as_refs, rs_refs, bonus_refs, gam_refs = zip(*scrs)

    zrows = jnp.zeros((HEAD_DIM, 2 * CHUNK), BF16)

    def head_rows(ref, pr, hh):
        rows = ref[pr * LANES + hh * HEAD_DIM:pr * LANES + (hh + 1) * HEAD_DIM, :]
        return jnp.concatenate([rows, zrows] if hh == 0 else [zrows, rows], axis=0)

    amats = [jnp.dot(lhs_refs[ck][:, sls[pr]], head_rows(rhs_refs[ck], pr, hh), preferred_element_type=F32)
             for ck, pr, hh in heads]
    yield
    n_mats = [jnp.where(strict, am[:CHUNK, :CHUNK], 0.0) for am in amats]
    avs, kvs = [], []
    for ck in range(n_ck):
        for pr in range(PAIRS):
            i0 = (ck * PAIRS + pr) * 2
            stacked = jnp.concatenate(
                [jnp.where(strict, amats[i0 + hh][:CHUNK, CHUNK:], 0.0).astype(BF16) for hh in range(2)]
                + [ht_refs[ck][RWKV_WIDTH + pr * LANES:RWKV_WIDTH + (pr + 1) * LANES, :]], axis=0)
            out = jnp.dot(stacked, vsw_refs[ck][:, sls[pr]], preferred_element_type=F32)
            avs += [out[:CHUNK], out[CHUNK:2 * CHUNK]]
            kvs.append(out[2 * CHUNK:])
    nds = [jnp.where(blk8, n, 0.0) for n in n_mats]
    ndbs = [nd.astype(BF16) for nd in nds]
    s2s = [jnp.dot(ndb, ndb, preferred_element_type=F32) for ndb in ndbs]
    anchors.append(_zero_row(s2s[-1]))
    yield
    ps = [jnp.where(eye, 1.0, nd) for nd in nds]
    pss = [_mm(jnp.concatenate([p, s2], axis=0), s2) for p, s2 in zip(ps, s2s)]
    yield
    ps = [p + x[:CHUNK] for p, x in zip(ps, pss)]
    dbs = [(p + _mm(p, x[CHUNK:])).astype(BF16) for p, x in zip(ps, pss)]
    yield
    for sh in (3, 4, 5):
        off = ((row >> (sh + 1)) == (col >> (sh + 1))) & ((row >> sh) != (col >> sh))
        ts = [jnp.dot(jnp.where(off, n, 0.0).astype(BF16), db, preferred_element_type=F32)
              for n, db in zip(n_mats, dbs)]
        if sh == 3:
            anchors.append(_zero_row(ts[-1]))
        yield
        us = [jnp.dot(db, t.astype(BF16), preferred_element_type=F32) for db, t in zip(dbs, ts)]
        dbs = [db + u.astype(BF16) for db, u in zip(dbs, us)]
        if sh == 4:
            anchors.append(_zero_row(us[-1]))
        yield
    half = CHUNK // 2
    y0s = [jnp.dot(db, jnp.where(owns[hh], as_refs[ck][:, sls[pr]], av).astype(BF16), preferred_element_type=F32)
           for (ck, pr, hh), db, av in zip(heads, dbs, avs)]
    anchors.append(_zero_row(y0s[-1]))
    yield
    y0bs = [y0.astype(BF16) for y0 in y0s]
    lo_half = lax.broadcasted_iota(jnp.int32, (half, CHUNK), 1) < half
    ws = [jnp.dot(jnp.where(lo_half, n[half:], 0.0).astype(BF16), y0b, preferred_element_type=F32)
          for n, y0b in zip(n_mats, y0bs)]
    yield
    xss = [jnp.concatenate(
        [y0b[:half],
         (y0[half:] + jnp.dot(db[half:], jnp.concatenate([y0b[:half], w.astype(BF16)], axis=0),
                              preferred_element_type=F32)).astype(BF16)], axis=0)
        for y0, y0b, db, w in zip(y0s, y0bs, dbs, ws)]
    yield
    zpad = jnp.zeros((LANES, CHUNK), BF16)
    zrs = [jnp.dot(jnp.concatenate([jnp.where(incl2, am[CHUNK:, :], 0.0).astype(BF16),
                                    jnp.concatenate([ht_refs[ck][sls[pr], :], zpad], axis=1)], axis=0),
                   jnp.concatenate([xs, vso_refs[ck][hh, :, sls[pr]]], axis=0), preferred_element_type=F32)
           for (ck, pr, hh), am, xs in zip(heads, amats, xss)]
    zs = [zr[:CHUNK] for zr in zrs]
    res = [zr[CHUNK:] for zr in zrs]
    yield
    ys = [[] for _ in range(n_ck)]
    for pr in range(PAIRS):
        sl = sls[pr]
        st = st_ref[pr]
        for ck in range(n_ck):
            i0 = (ck * PAIRS + pr) * 2
            z0, z1 = zs[i0], zs[i0 + 1]
            cat = jnp.concatenate([res[i0][:HEAD_DIM], res[i0 + 1][HEAD_DIM:]], axis=0)
            m_t = jnp.where(same_head, cat, 0.0) + jnp.where(eye, gam_refs[ck][0:1, sl], 0.0)
            g_t = jnp.where(same_head, 0.0, cat + kvs[ck * PAIRS + pr])
            q_hat = rs_refs[ck][:, sl] + jnp.where(lo, z0, z1)
            y_loc_sw = jnp.where(lo, z1, z0)
            qm = _mm(jnp.concatenate([q_hat, m_t], axis=0), st)
            ys[ck].append(qm[:CHUNK] + y_loc_sw)
            st = qm[CHUNK:] + g_t
        st_ref[pr] = st
    yield
    for ck in range(n_ck):
        rows = slice(ck * CHUNK, (ck + 1) * CHUNK)
        y = _swap_pairs(jnp.concatenate(ys[ck], axis=1))
        mean = _head_sum(y, ones_pair) * (1.0 / HEAD_DIM)
        dlt = y - mean
        var = _head_sum(dlt * dlt, ones_pair) * (1.0 / HEAD_DIM)
        yn = dlt * lax.rsqrt(var + GN_EPS) * gnw_ref[...] + gnb_ref[...]
        g = g_ref[0, rows, :]
        out_ref[0, rows, :] = ((yn + bonus_refs[ck][...]) * (g * _sigmoid(g))).astype(out_ref.dtype)
    yield


_STAGE_ORDER = "SF" * 14


def _rwkv_kernel(ps_ref, prev_ref, g_ref, mu_ref, w0_ref, wup_ref, a0_ref, aup_ref,
                 kkw_ref, ka_ref, rk_ref, gnw_ref, gnb_ref, ones_ref, tri_ref,
                 out_ref, st_ref, *chunk_scr, last):
    s = pl.program_id(1)
    sets = [chunk_scr[i * N_CHUNK_SCRATCH:(i + 1) * N_CHUNK_SCRATCH] for i in range(2 * CHUNKS_PER_STEP)]
    slot_a, slot_b = sets[:CHUNKS_PER_STEP], sets[CHUNKS_PER_STEP:]
    params = (mu_ref, w0_ref, wup_ref, a0_ref, aup_ref, kkw_ref, ka_ref, rk_ref)

    def step(fe_scrs, sv_scrs):
        fes = []
        anchors = [0.0]
        for ck in range(CHUNKS_PER_STEP if fe_scrs is not None else 0):
            if ck == 0:
                prev_last = prev_ref[0, 7:8, :] * jnp.where(s > 0, 1.0, 0.0)
            else:
                prev_last = ps_ref[0, ck * CHUNK - 1:ck * CHUNK, :]
            fes.append(_frontend(ps_ref, ck, prev_last, params, ones_ref[...], tri_ref[...], fe_scrs[ck], anchors))
        svs = []
        if sv_scrs is not None:
            svs.append(_solver(g_ref, gnw_ref, gnb_ref, ones_ref[...], out_ref, st_ref, sv_scrs, anchors))
        for key in _STAGE_ORDER:
            for gen in (fes if key == "F" else svs):
                next(gen)
        for gen in svs + fes:
            for _ in gen:
                pass

    @pl.when(s == 0)
    def _():
        st_ref[...] = jnp.zeros_like(st_ref)
        step(slot_a, None)

    slots = [slot_a, slot_b]
    for parity in range(2):
        @pl.when((s > 0) & (s < last) & ((s & 1) == parity))
        def _():
            step(slots[parity], slots[1 - parity])

    @pl.when(s == last)
    def _():
        step(None, slots[1 - last % 2])


def _rwkv(ps, g_r, mu, w0, wup_pad, a0, aup_pad, kkw, ka, rk, gnw, gnb, ones_pair, tri):
    bsz, seq, _ = ps.shape
    ns = seq // STEP_ROWS
    sub = STEP_ROWS // 8

    def vec(width):
        return pl.BlockSpec((1, width), lambda b, s: (0, 0))

    def fe_block(s):
        return jnp.minimum(s, ns - 1)

    def sv_block(s):
        return jnp.maximum(s - 1, 0)

    return pl.pallas_call(
        functools.partial(_rwkv_kernel, last=ns),
        grid=(bsz, ns + 1),
        in_specs=[
            pl.BlockSpec((1, STEP_ROWS, SHIFT_WIDTH), lambda b, s: (b, fe_block(s), 0)),
            pl.BlockSpec((1, 8, SHIFT_WIDTH), lambda b, s: (b, jnp.maximum(fe_block(s) * sub - 1, 0), 0)),
            pl.BlockSpec((1, STEP_ROWS, RWKV_WIDTH), lambda b, s: (b, sv_block(s), 0)),
            vec(SHIFT_WIDTH),
            vec(RWKV_WIDTH),
            pl.BlockSpec((LANES, RWKV_WIDTH), lambda b, s: (0, 0)),
            vec(RWKV_WIDTH),
            pl.BlockSpec((LANES, RWKV_WIDTH), lambda b, s: (0, 0)),
            vec(RWKV_WIDTH), vec(RWKV_WIDTH), vec(RWKV_WIDTH), vec(RWKV_WIDTH), vec(RWKV_WIDTH),
            pl.BlockSpec((LANES, LANES), lambda b, s: (0, 0)),
            pl.BlockSpec((CHUNK, CHUNK), lambda b, s: (0, 0)),
        ],
        out_specs=pl.BlockSpec((1, STEP_ROWS, RWKV_WIDTH), lambda b, s: (b, sv_block(s), 0)),
        out_shape=jax.ShapeDtypeStruct((bsz, seq, RWKV_WIDTH), BF16),
        scratch_shapes=[pltpu.VMEM((PAIRS, LANES, LANES), F32)]
        + [buf for _ in range(2 * CHUNKS_PER_STEP) for buf in _chunk_scratch()],
        compiler_params=pltpu.CompilerParams(
            dimension_semantics=("arbitrary", "arbitrary"), vmem_limit_bytes=VMEM_LIMIT),
        name="rwkv7_chunked",
    )(ps, ps, g_r, mu, w0, wup_pad, a0, aup_pad, kkw, ka, rk, gnw, gnb, ones_pair, tri)


def _attn_kernel(sinks_ref, q_ref, kvc_ref, kvp_ref, g_ref, o_ref):
    n = pl.program_id(1)
    kv = jnp.concatenate([kvp_ref[0], kvc_ref[0]], axis=0)
    k_nat = kv[:, :KV_WIDTH]
    v_nat = kv[:, KV_WIDTH:]
    ks = [k_nat.astype(BF16), pltpu.roll(k_nat, HEAD_DIM, 1).astype(BF16)]
    vts = [v_nat.T.astype(BF16), pltpu.roll(v_nat, HEAD_DIM, 1).T.astype(BF16)]
    si = lax.broadcasted_iota(jnp.int32, (2 * WINDOW, WINDOW), 0)
    qi = lax.broadcasted_iota(jnp.int32, (2 * WINDOW, WINDOW), 1)
    band = (si > qi) & (si <= qi + WINDOW)
    first_valid = jnp.where(n > 0, 0, WINDOW)
    lo = lax.broadcasted_iota(jnp.int32, (WINDOW, LANES), 1) < HEAD_DIM
    owns = [lo, jnp.logical_not(lo)]
    qs = q_ref[0] * (HEAD_DIM ** -0.5 * LOG2E)
    n_blk = ATTN_WIDTH // LANES
    group = n_blk // 2

    def scores(j):
        rows = slice(j * WINDOW, (j + 1) * WINDOW)
        keys = slice(j * WINDOW, (j + 2) * WINDOW)
        out = []
        for blk in range(n_blk):
            qb = qs[rows, blk * LANES:(blk + 1) * LANES]
            for hf in range(2):
                kk = ks[0 if blk // group == hf else 1][keys]
                out.append(lax.dot_general(kk, jnp.where(owns[hf], qb, 0.0).astype(BF16),
                                           (((1,), (1,)), ((), ())), preferred_element_type=F32))
        return out

    def finish(j, sc):
        rows = slice(j * WINDOW, (j + 1) * WINDOW)
        keys = slice(j * WINDOW, (j + 2) * WINDOW)
        mask = band & (si >= first_valid) if j == 0 else band
        outs = []
        for blk in range(n_blk):
            halves = []
            for hf in range(2):
                s = jnp.where(mask, sc[2 * blk + hf], -1e30)
                sink = sinks_ref[2 * blk + hf] * LOG2E
                m = jnp.maximum(jnp.max(s, axis=0, keepdims=True), sink)
                p = jnp.exp2(s - m)
                denom = jnp.sum(p, axis=0, keepdims=True) + jnp.exp2(sink - m)
                vt = vts[0 if blk // group == hf else 1][:, keys]
                halves.append(jnp.dot(vt, p.astype(BF16), preferred_element_type=F32) / denom)
            outs.append(jnp.concatenate([halves[0][:HEAD_DIM], halves[1][HEAD_DIM:]], axis=0).T)
        g = g_ref[0, rows, :]
        o_ref[0, rows, :] = (jnp.concatenate(outs, axis=1) * (g * _sigmoid(g))).astype(o_ref.dtype)

    sc = scores(0)
    for j in range(ATTN_Q_BLOCKS):
        nxt = scores(j + 1) if j + 1 < ATTN_Q_BLOCKS else None
        finish(j, sc)
        sc = nxt


def _attn(sinks, q, kv, g_a):
    bsz, seq, _ = q.shape
    nb = seq // ATTN_ROWS
    return pl.pallas_call(
        _attn_kernel,
        grid=(bsz, nb),
        in_specs=[
            pl.BlockSpec(memory_space=pltpu.SMEM),
            pl.BlockSpec((1, ATTN_ROWS, ATTN_WIDTH), lambda b, n: (b, n, 0)),
            pl.BlockSpec((1, ATTN_ROWS, 2 * KV_WIDTH), lambda b, n: (b, n, 0)),
            pl.BlockSpec((1, WINDOW, 2 * KV_WIDTH), lambda b, n: (b, jnp.maximum(n * ATTN_Q_BLOCKS - 1, 0), 0)),
            pl.BlockSpec((1, ATTN_ROWS, ATTN_WIDTH), lambda b, n: (b, n, 0)),
        ],
        out_specs=pl.BlockSpec((1, ATTN_ROWS, ATTN_WIDTH), lambda b, n: (b, n, 0)),
        out_shape=jax.ShapeDtypeStruct((bsz, seq, ATTN_WIDTH), BF16),
        compiler_params=pltpu.CompilerParams(
            dimension_semantics=("arbitrary", "arbitrary"), vmem_limit_bytes=VMEM_LIMIT),
        name="swa_sink",
    )(sinks, q, kv, kv, g_a)


def _final_kernel(ya_ref, yb_ref, x_ref, w_ref, fg_ref, o_ref):
    for rows in _sub_rows(x_ref.shape[0]):
        xn = _out_residual(ya_ref, yb_ref, x_ref, w_ref, rows)
        ms = jnp.mean(xn * xn, axis=-1, keepdims=True)
        o_ref[rows, :] = xn * lax.rsqrt(ms + NORM_EPS) * fg_ref[...]


def _final(ya, yb, x2, w_bf16, fgain):
    rows = x2.shape[0]
    tm = ROW_TILE
    row_spec = lambda w: pl.BlockSpec((tm, w), lambda i: (i, 0))
    return pl.pallas_call(
        _final_kernel,
        grid=(rows // tm,),
        in_specs=[row_spec(RWKV_WIDTH), row_spec(ATTN_WIDTH), row_spec(D_MODEL),
                  _resident((D_MODEL, D_MODEL)), _resident((1, D_MODEL))],
        out_specs=row_spec(D_MODEL),
        out_shape=jax.ShapeDtypeStruct((rows, D_MODEL), F32),
        compiler_params=pltpu.CompilerParams(
            dimension_semantics=("arbitrary",), vmem_limit_bytes=VMEM_LIMIT),
        name="outproj_final",
    )(ya, yb, x2, w_bf16, fgain)


def kernel(x, norm_gain, w_in, shift_mu, w0, w_up, a0, a_up, k_k, k_a, r_k, gn_w, gn_b, sinks, w_out, final_gain):
    bsz, seq, _ = x.shape
    depth = w_in.shape[0]
    rows = bsz * seq

    head_id = np.arange(LANES) // HEAD_DIM
    ones_pair = jnp.asarray(head_id[:, None] == head_id[None, :], dtype=BF16)
    tri = jnp.asarray(np.tril(np.ones((CHUNK, CHUNK), np.float32)), dtype=BF16)
    zpad = jnp.zeros((LORA, RWKV_WIDTH), F32)
    w_in_b = w_in.astype(BF16)
    w_out_b = w_out.astype(BF16)

    x2 = x.reshape(rows, D_MODEL)
    proj = _inproj(x2, norm_gain[0][None, :], w_in_b[0])
    for l in range(depth):
        ps, g_r, q, kv, g_a = proj
        wup_pad = jnp.concatenate([w_up[l], zpad], axis=0).astype(BF16)
        aup_pad = jnp.concatenate([zpad, a_up[l]], axis=0).astype(BF16)
        y_a = _rwkv(ps.reshape(bsz, seq, SHIFT_WIDTH), g_r.reshape(bsz, seq, RWKV_WIDTH),
                    shift_mu[l][None, :], w0[l][None, :], wup_pad, a0[l][None, :], aup_pad,
                    k_k[l][None, :], k_a[l][None, :], r_k[l].reshape(1, RWKV_WIDTH),
                    gn_w[l][None, :], gn_b[l][None, :], ones_pair, tri)
        y_b = _attn(sinks[l], q.reshape(bsz, seq, ATTN_WIDTH), kv.reshape(bsz, seq, 2 * KV_WIDTH),
                    g_a.reshape(bsz, seq, ATTN_WIDTH))
        y_a = y_a.reshape(rows, RWKV_WIDTH)
        y_b = y_b.reshape(rows, ATTN_WIDTH)
        if l + 1 < depth:
            x2, *proj = _mid(y_a, y_b, x2, w_out_b[l], norm_gain[l + 1][None, :], w_in_b[l + 1])
        else:
            x2 = _final(y_a, y_b, x2, w_out_b[l], final_gain[None, :])
    return x2.reshape(bsz, seq, D_MODEL)
```

```python
import functools

import numpy as np
import jax
import jax.numpy as jnp
from jax import lax
from jax.experimental import pallas as pl
from jax.experimental.pallas import tpu as pltpu

D_MODEL = 1024
HEAD_DIM = 64
RWKV_WIDTH = 512
ATTN_WIDTH = 512
KV_WIDTH = 128
LORA = 64
SHIFT_WIDTH = 3 * RWKV_WIDTH + 2 * LORA
IN_WIDTH = SHIFT_WIDTH + RWKV_WIDTH + ATTN_WIDTH + 2 * KV_WIDTH + ATTN_WIDTH
NORM_EPS = 1e-5
GN_EPS = 64e-5
WINDOW = 128
CHUNK = 128
LANES = 128
PAIRS = RWKV_WIDTH // LANES
CHUNKS_PER_STEP = 2
STEP_ROWS = CHUNKS_PER_STEP * CHUNK
ATTN_Q_BLOCKS = 8
ATTN_ROWS = ATTN_Q_BLOCKS * WINDOW
LOG2E = 1.4426950408889634
ROW_TILE = 512
ROW_TILE_FINAL = 1024
SUB_ROWS = 256
VMEM_LIMIT = 48 * 1024 * 1024

F32 = jnp.float32
BF16 = jnp.bfloat16


def _mm(a, b):
    return jnp.dot(a.astype(BF16), b.astype(BF16), preferred_element_type=F32)


def _sigmoid(x):
    return 1.0 / (1.0 + jnp.exp(-x))


PROJ_WIDTHS = (SHIFT_WIDTH, RWKV_WIDTH, ATTN_WIDTH, 2 * KV_WIDTH, ATTN_WIDTH)


def _sub_rows(tm):
    return [slice(r0, r0 + SUB_ROWS) for r0 in range(0, tm, SUB_ROWS)]


def _norm_project(xs, gain_ref, w_ref, out_refs):
    hs = []
    for x in xs:
        ms = jnp.mean(x * x, axis=-1, keepdims=True)
        hs.append((x * lax.rsqrt(ms + NORM_EPS) * gain_ref[...]).astype(BF16))
    for h, rows in zip(hs, _sub_rows(out_refs[0].shape[0])):
        c0 = 0
        for ref in out_refs:
            w = ref.shape[-1]
            ref[rows, :] = jnp.dot(h, w_ref[:, c0:c0 + w], preferred_element_type=F32)
            c0 += w


def _inproj_kernel(x_ref, gain_ref, w_ref, *out_refs):
    _norm_project([x_ref[rows, :] for rows in _sub_rows(x_ref.shape[0])], gain_ref, w_ref, out_refs)


def _resident(shape):
    return pl.BlockSpec(shape, lambda i: (0,) * len(shape), pipeline_mode=pl.Buffered(1))


def _inproj(x2, gain, w_bf16):
    rows = x2.shape[0]
    tm = ROW_TILE
    return pl.pallas_call(
        _inproj_kernel,
        grid=(rows // tm,),
        in_specs=[
            pl.BlockSpec((tm, D_MODEL), lambda i: (i, 0)),
            _resident((1, D_MODEL)),
            _resident((D_MODEL, IN_WIDTH)),
        ],
        out_specs=[pl.BlockSpec((tm, w), lambda i: (i, 0)) for w in PROJ_WIDTHS],
        out_shape=[jax.ShapeDtypeStruct((rows, w), F32) for w in PROJ_WIDTHS],
        compiler_params=pltpu.CompilerParams(
            dimension_semantics=("arbitrary",), vmem_limit_bytes=VMEM_LIMIT),
        name="inproj",
    )(x2, gain, w_bf16)


def _out_residual(ya_ref, yb_ref, x_ref, w_ref, rows):
    y = (jnp.dot(ya_ref[rows, :], w_ref[:RWKV_WIDTH, :], preferred_element_type=F32)
         + jnp.dot(yb_ref[rows, :], w_ref[RWKV_WIDTH:, :], preferred_element_type=F32))
    return x_ref[rows, :] + y


def _mid_kernel(ya_ref, yb_ref, x_ref, wo_ref, gain_ref, wi_ref, xo_ref, *out_refs):
    xs = []
    for rows in _sub_rows(x_ref.shape[0]):
        xn = _out_residual(ya_ref, yb_ref, x_ref, wo_ref, rows)
        xo_ref[rows, :] = xn
        xs.append(xn)
    _norm_project(xs, gain_ref, wi_ref, out_refs)


def _mid(ya, yb, x2, wo_bf16, gain, wi_bf16):
    rows = x2.shape[0]
    tm = ROW_TILE
    row_spec = lambda w: pl.BlockSpec((tm, w), lambda i: (i, 0))
    return pl.pallas_call(
        _mid_kernel,
        grid=(rows // tm,),
        in_specs=[row_spec(RWKV_WIDTH), row_spec(ATTN_WIDTH), row_spec(D_MODEL),
                  _resident((D_MODEL, D_MODEL)), _resident((1, D_MODEL)), _resident((D_MODEL, IN_WIDTH))],
        out_specs=[row_spec(D_MODEL)] + [row_spec(w) for w in PROJ_WIDTHS],
        out_shape=[jax.ShapeDtypeStruct((rows, D_MODEL), F32)]
        + [jax.ShapeDtypeStruct((rows, w), F32) for w in PROJ_WIDTHS],
        compiler_params=pltpu.CompilerParams(
            dimension_semantics=("arbitrary",), vmem_limit_bytes=VMEM_LIMIT),
        name="outproj_inproj",
    )(ya, yb, x2, wo_bf16, gain, wi_bf16)


def _head_sum(x, ones_pair):
    return jnp.concatenate(
        [_mm(x[:, pr * LANES:(pr + 1) * LANES], ones_pair) for pr in range(PAIRS)], axis=1)


def _swap_pairs(x):
    return jnp.concatenate(
        [pltpu.roll(x[:, pr * LANES:(pr + 1) * LANES], HEAD_DIM, 1) for pr in range(PAIRS)], axis=1)


def _chunk_scratch():
    return [
        pltpu.VMEM((2 * CHUNK, RWKV_WIDTH), BF16),
        pltpu.VMEM((RWKV_WIDTH, 2 * CHUNK), BF16),
        pltpu.VMEM((2 * RWKV_WIDTH, CHUNK), BF16),
        pltpu.VMEM((CHUNK, RWKV_WIDTH), BF16),
        pltpu.VMEM((2, CHUNK, RWKV_WIDTH), BF16),
        pltpu.VMEM((CHUNK, RWKV_WIDTH), F32),
        pltpu.VMEM((CHUNK, RWKV_WIDTH), F32),
        pltpu.VMEM((CHUNK, RWKV_WIDTH), F32),
        pltpu.VMEM((8, RWKV_WIDTH), F32),
    ]


N_CHUNK_SCRATCH = len(_chunk_scratch())


def _frontend(ps_ref, ck, prev_last, params, ones_pair, tri, scr, anchors):
    mu_ref, w0_ref, wup_ref, a0_ref, aup_ref, kkw_ref, ka_ref, rk_ref = params
    lhs_ref, rhs_ref, ht_ref, vsw_ref, vso_ref, as_ref, rs_ref, bonus_ref, gam_ref = scr
    rows = slice(ck * CHUNK, (ck + 1) * CHUNK)
    first_row = lax.broadcasted_iota(jnp.int32, (CHUNK, 1), 0) == 0
    lo = lax.broadcasted_iota(jnp.int32, (CHUNK, LANES), 1) < HEAD_DIM

    def shifted(cols, anchor):
        p = ps_ref[0, rows, cols]
        prev = jnp.where(first_row, prev_last[:, cols], pltpu.roll(p, 1, 0))
        return p + (prev - p) * (mu_ref[:, cols] + anchor)

    wa = shifted(slice(3 * RWKV_WIDTH, SHIFT_WIDTH), 0.0)
    wa_b = wa.astype(BF16)
    wt_b = jnp.tanh(wa).astype(BF16)
    yield
    lw = jnp.dot(wt_b, wup_ref[...], preferred_element_type=F32)
    la = jnp.dot(wa_b, aup_ref[...], preferred_element_type=F32)
    for pr in range(PAIRS):
        sl = slice(pr * LANES, (pr + 1) * LANES)
        anchor = anchors[-1]
        r = shifted(slice(pr * LANES, (pr + 1) * LANES), anchor)
        k = shifted(slice(RWKV_WIDTH + pr * LANES, RWKV_WIDTH + (pr + 1) * LANES), anchor)
        v = shifted(slice(2 * RWKV_WIDTH + pr * LANES, 2 * RWKV_WIDTH + (pr + 1) * LANES), anchor)
        kkraw = k * kkw_ref[:, sl]
        sq_b = (kkraw * kkraw).astype(BF16)
        ld = _sigmoid(w0_ref[:, sl] + lw[:, sl]) * float(-LOG2E * np.exp(-0.5))
        a = _sigmoid(a0_ref[:, sl] + la[:, sl])
        kp = k * (1.0 + (a - 1.0) * ka_ref[:, sl])
        rkp_b = (r * kp * rk_ref[:, sl]).astype(BF16)
        l_hi = ld.astype(BF16)
        l_lo = (ld - l_hi.astype(F32)).astype(BF16)
        yield
        sums = jnp.dot(jnp.concatenate([sq_b, rkp_b], axis=0), ones_pair, preferred_element_type=F32)
        cs = jnp.dot(tri, l_hi, preferred_element_type=F32) + jnp.dot(tri, l_lo, preferred_element_type=F32)
        yield
        bonus_ref[:, sl] = sums[CHUNK:] * v
        kk = kkraw * lax.rsqrt(jnp.maximum(sums[:CHUNK], 1e-24))
        bv = kk * a
        mid = cs[CHUNK // 2 - 1:CHUNK // 2, :]
        end = cs[CHUNK - 1:CHUNK, :]
        e_out = jnp.exp2(mid - cs)
        e_end = jnp.exp2(end - cs)
        a_t = -kk * jnp.exp2(cs - ld - mid)
        r_t = r * jnp.exp2(cs - mid)
        g_mid = jnp.exp2(mid)
        as_ref[:, sl] = a_t * g_mid
        rs_ref[:, sl] = r_t * g_mid
        gam_ref[:, sl] = jnp.broadcast_to(jnp.exp2(end), (gam_ref.shape[0], LANES))
        lhs_ref[:CHUNK, sl] = a_t.astype(BF16)
        lhs_ref[CHUNK:, sl] = r_t.astype(BF16)
        rhs_ref[sl, :CHUNK] = (bv * e_out).T.astype(BF16)
        rhs_ref[sl, CHUNK:] = (kp * e_out).T.astype(BF16)
        ht_ref[sl, :] = (bv * e_end).T.astype(BF16)
        ht_ref[RWKV_WIDTH + pr * LANES:RWKV_WIDTH + (pr + 1) * LANES, :] = (kp * e_end).T.astype(BF16)
        v_sw = pltpu.roll(v, HEAD_DIM, 1)
        vsw_ref[:, sl] = v_sw.astype(BF16)
        vso_ref[0, :, sl] = jnp.where(lo, 0.0, v_sw).astype(BF16)
        vso_ref[1, :, sl] = jnp.where(lo, v_sw, 0.0).astype(BF16)
        yield


def _zero_row(x):
    row = x[0:1, :]
    return jnp.where(row != row, 1.0, 0.0)


def _solver(g_ref, gnw_ref, gnb_ref, ones_pair, out_ref, st_ref, scrs, anchors):
    row = lax.broadcasted_iota(jnp.int32, (CHUNK, CHUNK), 0)
    col = lax.broadcasted_iota(jnp.int32, (CHUNK, CHUNK), 1)
    strict = col < row
    incl2 = jnp.concatenate([col <= row, col <= row], axis=1)
    lo = col < HEAD_DIM
    owns = [lo, jnp.logical_not(lo)]
    same_head = (row >> 6) == (col >> 6)
    eye = row == col
    blk8 = (row >> 3) == (col >> 3)
    n_ck = len(scrs)
    heads = [(ck, pr, hh) for ck in range(n_ck) for pr in range(PAIRS) for hh in range(2)]
    sls = [slice(pr * LANES, (pr + 1) * LANES) for pr in range(PAIRS)]
    lhs_refs, rhs_refs, ht_refs, vsw_refs, vso_refs, as_refs, rs_refs, bonus_refs, gam_refs = zip(*scrs)

    zrows = jnp.zeros((HEAD_DIM, 2 * CHUNK), BF16)

    def head_rows(ref, pr, hh):
        rows = ref[pr * LANES + hh * HEAD_DIM:pr * LANES + (hh + 1) * HEAD_DIM, :]
        return jnp.concatenate([rows, zrows] if hh == 0 else [zrows, rows], axis=0)

    amats = [jnp.dot(lhs_refs[ck][:, sls[pr]], head_rows(rhs_refs[ck], pr, hh), preferred_element_type=F32)
             for ck, pr, hh in heads]
    yield
    n_mats = [jnp.where(strict, am[:CHUNK, :CHUNK], 0.0) for am in amats]
    avs, kvs = [], []
    for ck in range(n_ck):
        for pr in range(PAIRS):
            i0 = (ck * PAIRS + pr) * 2
            stacked = jnp.concatenate(
                [jnp.where(strict, amats[i0 + hh][:CHUNK, CHUNK:], 0.0).astype(BF16) for hh in range(2)]
                + [ht_refs[ck][RWKV_WIDTH + pr * LANES:RWKV_WIDTH + (pr + 1) * LANES, :]], axis=0)
            out = jnp.dot(stacked, vsw_refs[ck][:, sls[pr]], preferred_element_type=F32)
            avs += [out[:CHUNK], out[CHUNK:2 * CHUNK]]
            kvs.append(out[2 * CHUNK:])
    nds = [jnp.where(blk8, n, 0.0) for n in n_mats]
    ndbs = [nd.astype(BF16) for nd in nds]
    s2s = [jnp.dot(ndb, ndb, preferred_element_type=F32) for ndb in ndbs]
    anchors.append(_zero_row(s2s[-1]))
    yield
    ps = [jnp.where(eye, 1.0, nd) for nd in nds]
    pss = [_mm(jnp.concatenate([p, s2], axis=0), s2) for p, s2 in zip(ps, s2s)]
    yield
    ps = [p + x[:CHUNK] for p, x in zip(ps, pss)]
    dbs = [(p + _mm(p, x[CHUNK:])).astype(BF16) for p, x in zip(ps, pss)]
    yield
    for sh in (3, 4, 5):
        off = ((row >> (sh + 1)) == (col >> (sh + 1))) & ((row >> sh) != (col >> sh))
        ts = [jnp.dot(jnp.where(off, n, 0.0).astype(BF16), db, preferred_element_type=F32)
              for n, db in zip(n_mats, dbs)]
        if sh == 3:
            anchors.append(_zero_row(ts[-1]))
        yield
        us = [jnp.dot(db, t.astype(BF16), preferred_element_type=F32) for db, t in zip(dbs, ts)]
        dbs = [db + u.astype(BF16) for db, u in zip(dbs, us)]
        if sh == 4:
            anchors.append(_zero_row(us[-1]))
        yield
    half = CHUNK // 2
    y0s = [jnp.dot(db, jnp.where(owns[hh], as_refs[ck][:, sls[pr]], av).astype(BF16), preferred_element_type=F32)
           for (ck, pr, hh), db, av in zip(heads, dbs, avs)]
    anchors.append(_zero_row(y0s[-1]))
    yield
    y0bs = [y0.astype(BF16) for y0 in y0s]
    lo_half = lax.broadcasted_iota(jnp.int32, (half, CHUNK), 1) < half
    ws = [jnp.dot(jnp.where(lo_half, n[half:], 0.0).astype(BF16), y0b, preferred_element_type=F32)
          for n, y0b in zip(n_mats, y0bs)]
    anchors.append(_zero_row(ws[-1]))
    yield
    xss = [jnp.concatenate(
        [y0b[:half],
         (y0[half:] + jnp.dot(db[half:], jnp.concatenate([y0b[:half], w.astype(BF16)], axis=0),
                              preferred_element_type=F32)).astype(BF16)], axis=0)
        for y0, y0b, db, w in zip(y0s, y0bs, dbs, ws)]
    yield
    zpad = jnp.zeros((LANES, CHUNK), BF16)
    zrs = [jnp.dot(jnp.concatenate([jnp.where(incl2, am[CHUNK:, :], 0.0).astype(BF16),
                                    jnp.concatenate([ht_refs[ck][sls[pr], :], zpad], axis=1)], axis=0),
                   jnp.concatenate([xs, vso_refs[ck][hh, :, sls[pr]]], axis=0), preferred_element_type=F32)
           for (ck, pr, hh), am, xs in zip(heads, amats, xss)]
    zs = [zr[:CHUNK] for zr in zrs]
    res = [zr[CHUNK:] for zr in zrs]
    anchors.append(_zero_row(zrs[-1]))
    yield
    ys = [[] for _ in range(n_ck)]
    for pr in range(PAIRS):
        sl = sls[pr]
        st = st_ref[pr]
        for ck in range(n_ck):
            i0 = (ck * PAIRS + pr) * 2
            z0, z1 = zs[i0], zs[i0 + 1]
            cat = jnp.concatenate([res[i0][:HEAD_DIM], res[i0 + 1][HEAD_DIM:]], axis=0)
            m_t = jnp.where(same_head, cat, 0.0) + jnp.where(eye, gam_refs[ck][0:1, sl], 0.0)
            g_t = jnp.where(same_head, 0.0, cat + kvs[ck * PAIRS + pr])
            q_hat = rs_refs[ck][:, sl] + jnp.where(lo, z0, z1)
            y_loc_sw = jnp.where(lo, z1, z0)
            qm = _mm(jnp.concatenate([q_hat, m_t], axis=0), st)
            ys[ck].append(qm[:CHUNK] + y_loc_sw)
            st = qm[CHUNK:] + g_t
        st_ref[pr] = st
    yield
    for ck in range(n_ck):
        rows = slice(ck * CHUNK, (ck + 1) * CHUNK)
        y = _swap_pairs(jnp.concatenate(ys[ck], axis=1))
        mean = _head_sum(y, ones_pair) * (1.0 / HEAD_DIM)
        dlt = y - mean
        var = _head_sum(dlt * dlt, ones_pair) * (1.0 / HEAD_DIM)
        yn = dlt * lax.rsqrt(var + GN_EPS) * gnw_ref[...] + gnb_ref[...]
        g = g_ref[0, rows, :]
        out_ref[0, rows, :] = ((yn + bonus_refs[ck][...]) * (g * _sigmoid(g))).astype(out_ref.dtype)
    yield


_STAGE_ORDER = "SSSS" + "FS" * 12


def _rwkv_kernel(ps_ref, prev_ref, g_ref, mu_ref, w0_ref, wup_ref, a0_ref, aup_ref,
                 kkw_ref, ka_ref, rk_ref, gnw_ref, gnb_ref, ones_ref, tri_ref,
                 out_ref, st_ref, *chunk_scr, last):
    s = pl.program_id(1)
    sets = [chunk_scr[i * N_CHUNK_SCRATCH:(i + 1) * N_CHUNK_SCRATCH] for i in range(2 * CHUNKS_PER_STEP)]
    slot_a, slot_b = sets[:CHUNKS_PER_STEP], sets[CHUNKS_PER_STEP:]
    params = (mu_ref, w0_ref, wup_ref, a0_ref, aup_ref, kkw_ref, ka_ref, rk_ref)

    def step(fe_scrs, sv_scrs):
        fes = []
        anchors = [0.0]
        for ck in range(CHUNKS_PER_STEP if fe_scrs is not None else 0):
            if ck == 0:
                prev_last = prev_ref[0, 7:8, :] * jnp.where(s > 0, 1.0, 0.0)
            else:
                prev_last = ps_ref[0, ck * CHUNK - 1:ck * CHUNK, :]
            fes.append(_frontend(ps_ref, ck, prev_last, params, ones_ref[...], tri_ref[...], fe_scrs[ck], anchors))
        svs = []
        if sv_scrs is not None:
            svs.append(_solver(g_ref, gnw_ref, gnb_ref, ones_ref[...], out_ref, st_ref, sv_scrs, anchors))
        for key in _STAGE_ORDER:
            for gen in (fes if key == "F" else svs):
                next(gen)
        for gen in svs + fes:
            for _ in gen:
                pass

    @pl.when(s == 0)
    def _():
        st_ref[...] = jnp.zeros_like(st_ref)
        step(slot_a, None)

    slots = [slot_a, slot_b]
    for parity in range(2):
        @pl.when((s > 0) & (s < last) & ((s & 1) == parity))
        def _():
            step(slots[parity], slots[1 - parity])

    @pl.when(s == last)
    def _():
        step(None, slots[1 - last % 2])


def _rwkv(ps, g_r, mu, w0, wup_pad, a0, aup_pad, kkw, ka, rk, gnw, gnb, ones_pair, tri):
    bsz, seq, _ = ps.shape
    ns = seq // STEP_ROWS
    sub = STEP_ROWS // 8

    def vec(width):
        return pl.BlockSpec((1, width), lambda b, s: (0, 0))

    def fe_block(s):
        return jnp.minimum(s, ns - 1)

    def sv_block(s):
        return jnp.maximum(s - 1, 0)

    return pl.pallas_call(
        functools.partial(_rwkv_kernel, last=ns),
        grid=(bsz, ns + 1),
        in_specs=[
            pl.BlockSpec((1, STEP_ROWS, SHIFT_WIDTH), lambda b, s: (b, fe_block(s), 0)),
            pl.BlockSpec((1, 8, SHIFT_WIDTH), lambda b, s: (b, jnp.maximum(fe_block(s) * sub - 1, 0), 0)),
            pl.BlockSpec((1, STEP_ROWS, RWKV_WIDTH), lambda b, s: (b, sv_block(s), 0)),
            vec(SHIFT_WIDTH),
            vec(RWKV_WIDTH),
            pl.BlockSpec((LANES, RWKV_WIDTH), lambda b, s: (0, 0)),
            vec(RWKV_WIDTH),
            pl.BlockSpec((LANES, RWKV_WIDTH), lambda b, s: (0, 0)),
            vec(RWKV_WIDTH), vec(RWKV_WIDTH), vec(RWKV_WIDTH), vec(RWKV_WIDTH), vec(RWKV_WIDTH),
            pl.BlockSpec((LANES, LANES), lambda b, s: (0, 0)),
            pl.BlockSpec((CHUNK, CHUNK), lambda b, s: (0, 0)),
        ],
        out_specs=pl.BlockSpec((1, STEP_ROWS, RWKV_WIDTH), lambda b, s: (b, sv_block(s), 0)),
        out_shape=jax.ShapeDtypeStruct((bsz, seq, RWKV_WIDTH), BF16),
        scratch_shapes=[pltpu.VMEM((PAIRS, LANES, LANES), F32)]
        + [buf for _ in range(2 * CHUNKS_PER_STEP) for buf in _chunk_scratch()],
        compiler_params=pltpu.CompilerParams(
            dimension_semantics=("arbitrary", "arbitrary"), vmem_limit_bytes=VMEM_LIMIT),
        name="rwkv7_chunked",
    )(ps, ps, g_r, mu, w0, wup_pad, a0, aup_pad, kkw, ka, rk, gnw, gnb, ones_pair, tri)


def _attn_kernel(sinks_ref, q_ref, kvc_ref, kvp_ref, g_ref, o_ref):
    n = pl.program_id(1)
    kv = jnp.concatenate([kvp_ref[0], kvc_ref[0]], axis=0)
    k_nat = kv[:, :KV_WIDTH]
    v_nat = kv[:, KV_WIDTH:]
    ks = [k_nat.astype(BF16), pltpu.roll(k_nat, HEAD_DIM, 1).astype(BF16)]
    vts = [v_nat.T.astype(BF16), pltpu.roll(v_nat, HEAD_DIM, 1).T.astype(BF16)]
    si = lax.broadcasted_iota(jnp.int32, (2 * WINDOW, WINDOW), 0)
    qi = lax.broadcasted_iota(jnp.int32, (2 * WINDOW, WINDOW), 1)
    band = (si > qi) & (si <= qi + WINDOW)
    first_valid = jnp.where(n > 0, 0, WINDOW)
    lo = lax.broadcasted_iota(jnp.int32, (WINDOW, LANES), 1) < HEAD_DIM
    owns = [lo, jnp.logical_not(lo)]
    qs = q_ref[0] * (HEAD_DIM ** -0.5 * LOG2E)
    n_blk = ATTN_WIDTH // LANES
    group = n_blk // 2

    def scores(j):
        rows = slice(j * WINDOW, (j + 1) * WINDOW)
        keys = slice(j * WINDOW, (j + 2) * WINDOW)
        out = []
        for blk in range(n_blk):
            qb = qs[rows, blk * LANES:(blk + 1) * LANES]
            for hf in range(2):
                kk = ks[0 if blk // group == hf else 1][keys]
                out.append(lax.dot_general(kk, jnp.where(owns[hf], qb, 0.0).astype(BF16),
                                           (((1,), (1,)), ((), ())), preferred_element_type=F32))
        return out

    def finish(j, sc):
        rows = slice(j * WINDOW, (j + 1) * WINDOW)
        keys = slice(j * WINDOW, (j + 2) * WINDOW)
        mask = band & (si >= first_valid) if j == 0 else band
        outs = []
        for blk in range(n_blk):
            halves = []
            for hf in range(2):
                s = jnp.where(mask, sc[2 * blk + hf], -1e30)
                sink = sinks_ref[2 * blk + hf] * LOG2E
                m = jnp.maximum(jnp.max(s, axis=0, keepdims=True), sink)
                p = jnp.exp2(s - m)
                denom = jnp.sum(p, axis=0, keepdims=True) + jnp.exp2(sink - m)
                vt = vts[0 if blk // group == hf else 1][:, keys]
                halves.append(jnp.dot(vt, p.astype(BF16), preferred_element_type=F32) / denom)
            outs.append(jnp.concatenate([halves[0][:HEAD_DIM], halves[1][HEAD_DIM:]], axis=0).T)
        g = g_ref[0, rows, :]
        o_ref[0, rows, :] = (jnp.concatenate(outs, axis=1) * (g * _sigmoid(g))).astype(o_ref.dtype)

    sc = scores(0)
    for j in range(ATTN_Q_BLOCKS):
        nxt = scores(j + 1) if j + 1 < ATTN_Q_BLOCKS else None
        finish(j, sc)
        sc = nxt


def _attn(sinks, q, kv, g_a):
    bsz, seq, _ = q.shape
    nb = seq // ATTN_ROWS
    return pl.pallas_call(
        _attn_kernel,
        grid=(bsz, nb),
        in_specs=[
            pl.BlockSpec(memory_space=pltpu.SMEM),
            pl.BlockSpec((1, ATTN_ROWS, ATTN_WIDTH), lambda b, n: (b, n, 0)),
            pl.BlockSpec((1, ATTN_ROWS, 2 * KV_WIDTH), lambda b, n: (b, n, 0)),
            pl.BlockSpec((1, WINDOW, 2 * KV_WIDTH), lambda b, n: (b, jnp.maximum(n * ATTN_Q_BLOCKS - 1, 0), 0)),
            pl.BlockSpec((1, ATTN_ROWS, ATTN_WIDTH), lambda b, n: (b, n, 0)),
        ],
        out_specs=pl.BlockSpec((1, ATTN_ROWS, ATTN_WIDTH), lambda b, n: (b, n, 0)),
        out_shape=jax.ShapeDtypeStruct((bsz, seq, ATTN_WIDTH), BF16),
        compiler_params=pltpu.CompilerParams(
            dimension_semantics=("arbitrary", "arbitrary"), vmem_limit_bytes=VMEM_LIMIT),
        name="swa_sink",
    )(sinks, q, kv, kv, g_a)


def _final_kernel(ya_ref, yb_ref, x_ref, w_ref, fg_ref, o_ref):
    for rows in _sub_rows(x_ref.shape[0]):
        xn = _out_residual(ya_ref, yb_ref, x_ref, w_ref, rows)
        ms = jnp.mean(xn * xn, axis=-1, keepdims=True)
        o_ref[rows, :] = xn * lax.rsqrt(ms + NORM_EPS) * fg_ref[...]


def _final(ya, yb, x2, w_bf16, fgain):
    rows = x2.shape[0]
    tm = ROW_TILE_FINAL
    row_spec = lambda w: pl.BlockSpec((tm, w), lambda i: (i, 0))
    return pl.pallas_call(
        _final_kernel,
        grid=(rows // tm,),
        in_specs=[row_spec(RWKV_WIDTH), row_spec(ATTN_WIDTH), row_spec(D_MODEL),
                  _resident((D_MODEL, D_MODEL)), _resident((1, D_MODEL))],
        out_specs=row_spec(D_MODEL),
        out_shape=jax.ShapeDtypeStruct((rows, D_MODEL), F32),
        compiler_params=pltpu.CompilerParams(
            dimension_semantics=("arbitrary",), vmem_limit_bytes=VMEM_LIMIT),
        name="outproj_final",
    )(ya, yb, x2, w_bf16, fgain)


def kernel(x, norm_gain, w_in, shift_mu, w0, w_up, a0, a_up, k_k, k_a, r_k, gn_w, gn_b, sinks, w_out, final_gain):
    bsz, seq, _ = x.shape
    depth = w_in.shape[0]
    rows = bsz * seq

    head_id = np.arange(LANES) // HEAD_DIM
    ones_pair = jnp.asarray(head_id[:, None] == head_id[None, :], dtype=BF16)
    tri = jnp.asarray(np.tril(np.ones((CHUNK, CHUNK), np.float32)), dtype=BF16)
    zpad = jnp.zeros((LORA, RWKV_WIDTH), F32)
    w_in_b = w_in.astype(BF16)
    w_out_b = w_out.astype(BF16)

    x2 = x.reshape(rows, D_MODEL)
    proj = _inproj(x2, norm_gain[0][None, :], w_in_b[0])
    for l in range(depth):
        ps, g_r, q, kv, g_a = proj
        wup_pad = jnp.concatenate([w_up[l], zpad], axis=0).astype(BF16)
        aup_pad = jnp.concatenate([zpad, a_up[l]], axis=0).astype(BF16)
        y_a = _rwkv(ps.reshape(bsz, seq, SHIFT_WIDTH), g_r.reshape(bsz, seq, RWKV_WIDTH),
                    shift_mu[l][None, :], w0[l][None, :], wup_pad, a0[l][None, :], aup_pad,
                    k_k[l][None, :], k_a[l][None, :], r_k[l].reshape(1, RWKV_WIDTH),
                    gn_w[l][None, :], gn_b[l][None, :], ones_pair, tri)
        y_b = _attn(sinks[l], q.reshape(bsz, seq, ATTN_WIDTH), kv.reshape(bsz, seq, 2 * KV_WIDTH),
                    g_a.reshape(bsz, seq, ATTN_WIDTH))
        y_a = y_a.reshape(rows, RWKV_WIDTH)
        y_b = y_b.reshape(rows, ATTN_WIDTH)
        if l + 1 < depth:
            x2, *proj = _mid(y_a, y_b, x2, w_out_b[l], norm_gain[l + 1][None, :], w_in_b[l + 1])
        else:
            x2 = _final(y_a, y_b, x2, w_out_b[l], final_gain[None, :])
    return x2.reshape(bsz, seq, D_MODEL)
```

```python
import functools

import numpy as np
import jax
import jax.numpy as jnp
from jax import lax
from jax.experimental import pallas as pl
from jax.experimental.pallas import tpu as pltpu

D_MODEL = 1024
HEAD_DIM = 64
RWKV_WIDTH = 512
ATTN_WIDTH = 512
KV_WIDTH = 128
LORA = 64
SHIFT_WIDTH = 3 * RWKV_WIDTH + 2 * LORA
IN_WIDTH = SHIFT_WIDTH + RWKV_WIDTH + ATTN_WIDTH + 2 * KV_WIDTH + ATTN_WIDTH
NORM_EPS = 1e-5
GN_EPS = 64e-5
WINDOW = 128
CHUNK = 128
LANES = 128
PAIRS = RWKV_WIDTH // LANES
CHUNKS_PER_STEP = 2
STEP_ROWS = CHUNKS_PER_STEP * CHUNK
ATTN_Q_BLOCKS = 8
ATTN_ROWS = ATTN_Q_BLOCKS * WINDOW
LOG2E = 1.4426950408889634
ROW_TILE = 512
ROW_TILE_FINAL = 1024
SUB_ROWS = 256
VMEM_LIMIT = 48 * 1024 * 1024

F32 = jnp.float32
BF16 = jnp.bfloat16


def _mm(a, b):
    return jnp.dot(a.astype(BF16), b.astype(BF16), preferred_element_type=F32)


def _sigmoid(x):
    return 1.0 / (1.0 + jnp.exp(-x))


PROJ_WIDTHS = (SHIFT_WIDTH, RWKV_WIDTH, ATTN_WIDTH, 2 * KV_WIDTH, ATTN_WIDTH)


def _sub_rows(tm):
    return [slice(r0, r0 + SUB_ROWS) for r0 in range(0, tm, SUB_ROWS)]


def _norm_project(xs, gain_ref, w_ref, out_refs):
    hs = []
    for x in xs:
        ms = jnp.mean(x * x, axis=-1, keepdims=True)
        hs.append((x * lax.rsqrt(ms + NORM_EPS) * gain_ref[...]).astype(BF16))
    for h, rows in zip(hs, _sub_rows(out_refs[0].shape[0])):
        c0 = 0
        for ref in out_refs:
            w = ref.shape[-1]
            ref[rows, :] = jnp.dot(h, w_ref[:, c0:c0 + w], preferred_element_type=F32)
            c0 += w


def _inproj_kernel(x_ref, gain_ref, w_ref, *out_refs):
    _norm_project([x_ref[rows, :] for rows in _sub_rows(x_ref.shape[0])], gain_ref, w_ref, out_refs)


def _resident(shape):
    return pl.BlockSpec(shape, lambda i: (0,) * len(shape), pipeline_mode=pl.Buffered(1))


def _inproj(x2, gain, w_bf16):
    rows = x2.shape[0]
    tm = ROW_TILE
    return pl.pallas_call(
        _inproj_kernel,
        grid=(rows // tm,),
        in_specs=[
            pl.BlockSpec((tm, D_MODEL), lambda i: (i, 0)),
            _resident((1, D_MODEL)),
            _resident((D_MODEL, IN_WIDTH)),
        ],
        out_specs=[pl.BlockSpec((tm, w), lambda i: (i, 0)) for w in PROJ_WIDTHS],
        out_shape=[jax.ShapeDtypeStruct((rows, w), F32) for w in PROJ_WIDTHS],
        compiler_params=pltpu.CompilerParams(
            dimension_semantics=("arbitrary",), vmem_limit_bytes=VMEM_LIMIT),
        name="inproj",
    )(x2, gain, w_bf16)


def _out_residual(ya_ref, yb_ref, x_ref, w_ref, rows):
    y = (jnp.dot(ya_ref[rows, :], w_ref[:RWKV_WIDTH, :], preferred_element_type=F32)
         + jnp.dot(yb_ref[rows, :], w_ref[RWKV_WIDTH:, :], preferred_element_type=F32))
    return x_ref[rows, :] + y


def _mid_kernel(ya_ref, yb_ref, x_ref, wo_ref, gain_ref, wi_ref, xo_ref, *out_refs):
    xs = []
    for rows in _sub_rows(x_ref.shape[0]):
        xn = _out_residual(ya_ref, yb_ref, x_ref, wo_ref, rows)
        xo_ref[rows, :] = xn
        xs.append(xn)
    _norm_project(xs, gain_ref, wi_ref, out_refs)


def _mid(ya, yb, x2, wo_bf16, gain, wi_bf16):
    rows = x2.shape[0]
    tm = ROW_TILE
    row_spec = lambda w: pl.BlockSpec((tm, w), lambda i: (i, 0))
    return pl.pallas_call(
        _mid_kernel,
        grid=(rows // tm,),
        in_specs=[row_spec(RWKV_WIDTH), row_spec(ATTN_WIDTH), row_spec(D_MODEL),
                  _resident((D_MODEL, D_MODEL)), _resident((1, D_MODEL)), _resident((D_MODEL, IN_WIDTH))],
        out_specs=[row_spec(D_MODEL)] + [row_spec(w) for w in PROJ_WIDTHS],
        out_shape=[jax.ShapeDtypeStruct((rows, D_MODEL), F32)]
        + [jax.ShapeDtypeStruct((rows, w), F32) for w in PROJ_WIDTHS],
        compiler_params=pltpu.CompilerParams(
            dimension_semantics=("arbitrary",), vmem_limit_bytes=VMEM_LIMIT),
        name="outproj_inproj",
    )(ya, yb, x2, wo_bf16, gain, wi_bf16)


def _chunk_scratch():
    return [
        pltpu.VMEM((2 * CHUNK, RWKV_WIDTH), BF16),
        pltpu.VMEM((RWKV_WIDTH, 2 * CHUNK), BF16),
        pltpu.VMEM((2 * RWKV_WIDTH, CHUNK), BF16),
        pltpu.VMEM((CHUNK, RWKV_WIDTH), BF16),
        pltpu.VMEM((2, CHUNK, RWKV_WIDTH), BF16),
        pltpu.VMEM((CHUNK, RWKV_WIDTH), F32),
        pltpu.VMEM((CHUNK, RWKV_WIDTH), F32),
        pltpu.VMEM((CHUNK, RWKV_WIDTH), F32),
        pltpu.VMEM((8, RWKV_WIDTH), F32),
    ]


N_CHUNK_SCRATCH = len(_chunk_scratch())


def _token_shift(ps_ref, rows, cols, prev_last, mu_ref, anchor):
    first_row = lax.broadcasted_iota(jnp.int32, (CHUNK, 1), 0) == 0
    p = ps_ref[0, rows, cols]
    prev = jnp.where(first_row, prev_last[:, cols], pltpu.roll(p, 1, 0))
    return p + (prev - p) * (mu_ref[:, cols] + anchor)


def _frontend(ps_ref, ck, prev_last, lw, la, params, ones_pair, tri, scr, anchors):
    mu_ref, w0_ref, a0_ref, kkw_ref, ka_ref, rk_ref = params
    lhs_ref, rhs_ref, ht_ref, vsw_ref, vso_ref, as_ref, rs_ref, bonus_ref, gam_ref = scr
    rows = slice(ck * CHUNK, (ck + 1) * CHUNK)
    lo = lax.broadcasted_iota(jnp.int32, (CHUNK, LANES), 1) < HEAD_DIM

    def shifted(cols, anchor):
        return _token_shift(ps_ref, rows, cols, prev_last, mu_ref, anchor)

    for pr in range(PAIRS):
        sl = slice(pr * LANES, (pr + 1) * LANES)
        anchor = anchors[-1]
        r = shifted(slice(pr * LANES, (pr + 1) * LANES), anchor)
        k = shifted(slice(RWKV_WIDTH + pr * LANES, RWKV_WIDTH + (pr + 1) * LANES), anchor)
        v = shifted(slice(2 * RWKV_WIDTH + pr * LANES, 2 * RWKV_WIDTH + (pr + 1) * LANES), anchor)
        kkraw = k * kkw_ref[:, sl]
        sq_b = (kkraw * kkraw).astype(BF16)
        ld = _sigmoid(w0_ref[:, sl] + lw[:, sl]) * float(-LOG2E * np.exp(-0.5))
        a = _sigmoid(a0_ref[:, sl] + la[:, sl])
        kp = k * (1.0 + (a - 1.0) * ka_ref[:, sl])
        rkp_b = (r * kp * rk_ref[:, sl]).astype(BF16)
        l_hi = ld.astype(BF16)
        l_lo = (ld - l_hi.astype(F32)).astype(BF16)
        yield
        sums = jnp.dot(jnp.concatenate([sq_b, rkp_b], axis=0), ones_pair, preferred_element_type=F32)
        cs = jnp.dot(tri, l_hi, preferred_element_type=F32) + jnp.dot(tri, l_lo, preferred_element_type=F32)
        yield
        bonus_ref[:, sl] = sums[CHUNK:] * v
        kk = kkraw * lax.rsqrt(jnp.maximum(sums[:CHUNK], 1e-24))
        bv = kk * a
        mid = cs[CHUNK // 2 - 1:CHUNK // 2, :]
        end = cs[CHUNK - 1:CHUNK, :]
        e_out = jnp.exp2(mid - cs)
        e_end = jnp.exp2(end - cs)
        a_t = -kk * jnp.exp2(cs - ld - mid)
        r_t = r * jnp.exp2(cs - mid)
        g_mid = jnp.exp2(mid)
        as_ref[:, sl] = a_t * g_mid
        rs_ref[:, sl] = r_t * g_mid
        gam_ref[:, sl] = jnp.broadcast_to(jnp.exp2(end), (gam_ref.shape[0], LANES))
        lhs_ref[:CHUNK, sl] = a_t.astype(BF16)
        lhs_ref[CHUNK:, sl] = r_t.astype(BF16)
        rhs_ref[sl, :CHUNK] = (bv * e_out).T.astype(BF16)
        rhs_ref[sl, CHUNK:] = (kp * e_out).T.astype(BF16)
        ht_ref[sl, :] = (bv * e_end).T.astype(BF16)
        ht_ref[RWKV_WIDTH + pr * LANES:RWKV_WIDTH + (pr + 1) * LANES, :] = (kp * e_end).T.astype(BF16)
        v_sw = pltpu.roll(v, HEAD_DIM, 1)
        vsw_ref[:, sl] = v_sw.astype(BF16)
        vso_ref[0, :, sl] = jnp.where(lo, 0.0, v_sw).astype(BF16)
        vso_ref[1, :, sl] = jnp.where(lo, v_sw, 0.0).astype(BF16)
        yield


def _odd_rows(x, sh):
    k = 1 << sh
    return jnp.concatenate([x[r0:r0 + k] for r0 in range(k, CHUNK, 2 * k)], axis=0)


def _spread_odd_rows(xc, sh):
    k = 1 << sh
    zero = jnp.zeros((k, xc.shape[1]), xc.dtype)
    pieces = []
    for i in range(CHUNK // (2 * k)):
        pieces += [zero, xc[i * k:(i + 1) * k]]
    return jnp.concatenate(pieces, axis=0)


def _zero_row(x):
    row = x[0:1, :]
    return jnp.where(row != row, 1.0, 0.0)


def _solver(g_ref, gnw_ref, gnb_ref, ones_pair, out_ref, st_ref, scrs, anchors):
    row = lax.broadcasted_iota(jnp.int32, (CHUNK, CHUNK), 0)
    col = lax.broadcasted_iota(jnp.int32, (CHUNK, CHUNK), 1)
    strict = col < row
    incl2 = jnp.concatenate([col <= row, col <= row], axis=1)
    lo = col < HEAD_DIM
    owns = [lo, jnp.logical_not(lo)]
    same_head = (row >> 6) == (col >> 6)
    eye = row == col
    blk8 = (row >> 3) == (col >> 3)
    n_ck = len(scrs)
    heads = [(ck, pr, hh) for ck in range(n_ck) for pr in range(PAIRS) for hh in range(2)]
    sls = [slice(pr * LANES, (pr + 1) * LANES) for pr in range(PAIRS)]
    lhs_refs, rhs_refs, ht_refs, vsw_refs, vso_refs, as_refs, rs_refs, bonus_refs, gam_refs = zip(*scrs)

    zrows = jnp.zeros((HEAD_DIM, 2 * CHUNK), BF16)

    def head_rows(ref, pr, hh):
        rows = ref[pr * LANES + hh * HEAD_DIM:pr * LANES + (hh + 1) * HEAD_DIM, :]
        return jnp.concatenate([rows, zrows] if hh == 0 else [zrows, rows], axis=0)

    amats = [jnp.dot(lhs_refs[ck][:, sls[pr]], head_rows(rhs_refs[ck], pr, hh), preferred_element_type=F32)
             for ck, pr, hh in heads]
    yield
    n_mats = [jnp.where(strict, am[:CHUNK, :CHUNK], 0.0) for am in amats]
    avs, kvs = [], []
    for ck in range(n_ck):
        for pr in range(PAIRS):
            i0 = (ck * PAIRS + pr) * 2
            stacked = jnp.concatenate(
                [jnp.where(strict, amats[i0 + hh][:CHUNK, CHUNK:], 0.0).astype(BF16) for hh in range(2)]
                + [ht_refs[ck][RWKV_WIDTH + pr * LANES:RWKV_WIDTH + (pr + 1) * LANES, :]], axis=0)
            out = jnp.dot(stacked, vsw_refs[ck][:, sls[pr]], preferred_element_type=F32)
            avs += [out[:CHUNK], out[CHUNK:2 * CHUNK]]
            kvs.append(out[2 * CHUNK:])
    nds = [jnp.where(blk8, n, 0.0) for n in n_mats]
    ndbs = [nd.astype(BF16) for nd in nds]
    s2s = [jnp.dot(ndb, ndb, preferred_element_type=F32) for ndb in ndbs]
    anchors.append(_zero_row(s2s[-1]))
    yield
    ps = [jnp.where(eye, 1.0, nd) for nd in nds]
    pss = [_mm(jnp.concatenate([p, s2], axis=0), s2) for p, s2 in zip(ps, s2s)]
    yield
    ps = [p + x[:CHUNK] for p, x in zip(ps, pss)]
    d8s = [p + _mm(p, x[CHUNK:]) for p, x in zip(ps, pss)]
    dbs = [d.astype(BF16) for d in d8s]
    yield
    for sh in (3, 4, 5):
        off = ((row >> (sh + 1)) == (col >> (sh + 1))) & ((row >> sh) != (col >> sh))
        ts = [jnp.dot(_odd_rows(jnp.where(off, n, 0.0), sh).astype(BF16), db, preferred_element_type=F32)
              for n, db in zip(n_mats, dbs)]
        if sh == 3:
            anchors.append(_zero_row(ts[-1]))
        yield
        d_odd = [_odd_rows(d8, sh).astype(BF16) for d8 in d8s] if sh == 3 else [_odd_rows(db, sh) for db in dbs]
        us = [jnp.dot(do, _spread_odd_rows(t, sh).astype(BF16), preferred_element_type=F32)
              for do, t in zip(d_odd, ts)]
        dbs = [db + _spread_odd_rows(u, sh).astype(BF16) for db, u in zip(dbs, us)]
        if sh == 4:
            anchors.append(_zero_row(us[-1]))
        yield
    half = CHUNK // 2
    y0s = [jnp.dot(db, jnp.where(owns[hh], as_refs[ck][:, sls[pr]], av).astype(BF16), preferred_element_type=F32)
           for (ck, pr, hh), db, av in zip(heads, dbs, avs)]
    anchors.append(_zero_row(y0s[-1]))
    yield
    y0bs = [y0.astype(BF16) for y0 in y0s]
    lo_half = lax.broadcasted_iota(jnp.int32, (half, CHUNK), 1) < half
    ws = [jnp.dot(jnp.where(lo_half, n[half:], 0.0).astype(BF16), y0b, preferred_element_type=F32)
          for n, y0b in zip(n_mats, y0bs)]
    anchors.append(_zero_row(ws[-1]))
    yield
    xss = [jnp.concatenate(
        [y0b[:half],
         (y0[half:] + jnp.dot(db[half:], jnp.concatenate([y0b[:half], w.astype(BF16)], axis=0),
                              preferred_element_type=F32)).astype(BF16)], axis=0)
        for y0, y0b, db, w in zip(y0s, y0bs, dbs, ws)]
    yield
    zpad = jnp.zeros((LANES, CHUNK), BF16)
    zrs = [jnp.dot(jnp.concatenate([jnp.where(incl2, am[CHUNK:, :], 0.0).astype(BF16),
                                    jnp.concatenate([ht_refs[ck][sls[pr], :], zpad], axis=1)], axis=0),
                   jnp.concatenate([xs, vso_refs[ck][hh, :, sls[pr]]], axis=0), preferred_element_type=F32)
           for (ck, pr, hh), am, xs in zip(heads, amats, xss)]
    zs = [zr[:CHUNK] for zr in zrs]
    res = [zr[CHUNK:] for zr in zrs]
    anchors.append(_zero_row(zrs[-1]))
    yield
    ys = [[] for _ in range(n_ck)]
    for pr in range(PAIRS):
        sl = sls[pr]
        st = st_ref[pr]
        for ck in range(n_ck):
            i0 = (ck * PAIRS + pr) * 2
            z0, z1 = zs[i0], zs[i0 + 1]
            cat = jnp.concatenate([res[i0][:HEAD_DIM], res[i0 + 1][HEAD_DIM:]], axis=0)
            m_t = jnp.where(same_head, cat, 0.0) + jnp.where(eye, gam_refs[ck][0:1, sl], 0.0)
            g_t = jnp.where(same_head, 0.0, cat + kvs[ck * PAIRS + pr])
            q_hat = rs_refs[ck][:, sl] + jnp.where(lo, z0, z1)
            y_loc_sw = jnp.where(lo, z1, z0)
            qm = _mm(jnp.concatenate([q_hat, m_t], axis=0), st)
            ys[ck].append(qm[:CHUNK] + y_loc_sw)
            st = qm[CHUNK:] + g_t
        st_ref[pr] = st
    yield
    y_all = jnp.concatenate([pltpu.roll(y_sw, HEAD_DIM, 1) for ck in range(n_ck) for y_sw in ys[ck]], axis=0)
    mean = _mm(y_all, ones_pair) * (1.0 / HEAD_DIM)
    dlt = y_all - mean
    var = _mm(dlt * dlt, ones_pair) * (1.0 / HEAD_DIM)
    yn_all = dlt * lax.rsqrt(var + GN_EPS)
    for ck in range(n_ck):
        rows = slice(ck * CHUNK, (ck + 1) * CHUNK)
        for pr in range(PAIRS):
            sl = sls[pr]
            blk = slice((ck * PAIRS + pr) * CHUNK, (ck * PAIRS + pr + 1) * CHUNK)
            g = g_ref[0, rows, sl]
            yn = yn_all[blk] * gnw_ref[:, sl] + gnb_ref[:, sl]
            out_ref[0, rows, sl] = ((yn + bonus_refs[ck][:, sl]) * (g * _sigmoid(g))).astype(out_ref.dtype)
    yield


_SOLVER_LEAD = 4
_ALTERNATIONS = 12


def _rwkv_kernel(ps_ref, prev_ref, g_ref, mu_ref, w0_ref, wup_ref, a0_ref, aup_ref,
                 kkw_ref, ka_ref, rk_ref, gnw_ref, gnb_ref, ones_ref, tri_ref,
                 out_ref, st_ref, *chunk_scr, last):
    s = pl.program_id(1)
    sets = [chunk_scr[i * N_CHUNK_SCRATCH:(i + 1) * N_CHUNK_SCRATCH] for i in range(2 * CHUNKS_PER_STEP)]
    slot_a, slot_b = sets[:CHUNKS_PER_STEP], sets[CHUNKS_PER_STEP:]
    params = (mu_ref, w0_ref, a0_ref, kkw_ref, ka_ref, rk_ref)

    def step(fe_scrs, sv_scrs):
        anchors = [0.0]
        svs = []
        if sv_scrs is not None:
            svs.append(_solver(g_ref, gnw_ref, gnb_ref, ones_ref[...], out_ref, st_ref, sv_scrs, anchors))
        fes = []
        if fe_scrs is not None:
            prev_lasts = [prev_ref[0, 7:8, :] * jnp.where(s > 0, 1.0, 0.0)]
            prev_lasts += [ps_ref[0, ck * CHUNK - 1:ck * CHUNK, :] for ck in range(1, CHUNKS_PER_STEP)]
            wa = jnp.concatenate(
                [_token_shift(ps_ref, slice(ck * CHUNK, (ck + 1) * CHUNK), slice(3 * RWKV_WIDTH, SHIFT_WIDTH),
                              prev_lasts[ck], mu_ref, 0.0) for ck in range(CHUNKS_PER_STEP)], axis=0)
            wa_b = wa.astype(BF16)
            wt_b = jnp.tanh(wa).astype(BF16)
        for gen in svs * _SOLVER_LEAD:
            next(gen)
        if fe_scrs is not None:
            lw = jnp.dot(wt_b, wup_ref[...], preferred_element_type=F32)
            la = jnp.dot(wa_b, aup_ref[...], preferred_element_type=F32)
            for ck in range(CHUNKS_PER_STEP):
                rows = slice(ck * CHUNK, (ck + 1) * CHUNK)
                fes.append(_frontend(ps_ref, ck, prev_lasts[ck], lw[rows], la[rows], params, ones_ref[...],
                                     tri_ref[...], fe_scrs[ck], anchors))
        for _ in range(_ALTERNATIONS):
            for gen in fes + svs:
                next(gen, None)
        for gen in svs + fes:
            for _ in gen:
                pass

    @pl.when(s == 0)
    def _():
        st_ref[...] = jnp.zeros_like(st_ref)
        step(slot_a, None)

    slots = [slot_a, slot_b]
    for parity in range(2):
        @pl.when((s > 0) & (s < last) & ((s & 1) == parity))
        def _():
            step(slots[parity], slots[1 - parity])

    @pl.when(s == last)
    def _():
        step(None, slots[1 - last % 2])


def _rwkv(ps, g_r, mu, w0, wup_pad, a0, aup_pad, kkw, ka, rk, gnw, gnb, ones_pair, tri):
    bsz, seq, _ = ps.shape
    ns = seq // STEP_ROWS
    sub = STEP_ROWS // 8

    def vec(width):
        return pl.BlockSpec((1, width), lambda b, s: (0, 0))

    def fe_block(s):
        return jnp.minimum(s, ns - 1)

    def sv_block(s):
        return jnp.maximum(s - 1, 0)

    return pl.pallas_call(
        functools.partial(_rwkv_kernel, last=ns),
        grid=(bsz, ns + 1),
        in_specs=[
            pl.BlockSpec((1, STEP_ROWS, SHIFT_WIDTH), lambda b, s: (b, fe_block(s), 0)),
            pl.BlockSpec((1, 8, SHIFT_WIDTH), lambda b, s: (b, jnp.maximum(fe_block(s) * sub - 1, 0), 0)),
            pl.BlockSpec((1, STEP_ROWS, RWKV_WIDTH), lambda b, s: (b, sv_block(s), 0)),
            vec(SHIFT_WIDTH),
            vec(RWKV_WIDTH),
            pl.BlockSpec((LANES, RWKV_WIDTH), lambda b, s: (0, 0)),
            vec(RWKV_WIDTH),
            pl.BlockSpec((LANES, RWKV_WIDTH), lambda b, s: (0, 0)),
            vec(RWKV_WIDTH), vec(RWKV_WIDTH), vec(RWKV_WIDTH), vec(RWKV_WIDTH), vec(RWKV_WIDTH),
            pl.BlockSpec((LANES, LANES), lambda b, s: (0, 0)),
            pl.BlockSpec((CHUNK, CHUNK), lambda b, s: (0, 0)),
        ],
        out_specs=pl.BlockSpec((1, STEP_ROWS, RWKV_WIDTH), lambda b, s: (b, sv_block(s), 0)),
        out_shape=jax.ShapeDtypeStruct((bsz, seq, RWKV_WIDTH), BF16),
        scratch_shapes=[pltpu.VMEM((PAIRS, LANES, LANES), F32)]
        + [buf for _ in range(2 * CHUNKS_PER_STEP) for buf in _chunk_scratch()],
        compiler_params=pltpu.CompilerParams(
            dimension_semantics=("arbitrary", "arbitrary"), vmem_limit_bytes=VMEM_LIMIT),
        name="rwkv7_chunked",
    )(ps, ps, g_r, mu, w0, wup_pad, a0, aup_pad, kkw, ka, rk, gnw, gnb, ones_pair, tri)


def _attn_kernel(sinks_ref, q_ref, kvc_ref, kvp_ref, g_ref, o_ref):
    n = pl.program_id(1)
    kv = jnp.concatenate([kvp_ref[0], kvc_ref[0]], axis=0)
    k_nat = kv[:, :KV_WIDTH]
    v_nat = kv[:, KV_WIDTH:]
    ks = [k_nat.astype(BF16), pltpu.roll(k_nat, HEAD_DIM, 1).astype(BF16)]
    vts = [v_nat.T.astype(BF16), pltpu.roll(v_nat, HEAD_DIM, 1).T.astype(BF16)]
    ones_rows = jnp.ones((HEAD_DIM, vts[0].shape[1]), BF16)
    vts = [[jnp.concatenate([vt[:HEAD_DIM], ones_rows], axis=0), jnp.concatenate([ones_rows, vt[HEAD_DIM:]], axis=0)]
           for vt in vts]
    si = lax.broadcasted_iota(jnp.int32, (2 * WINDOW, WINDOW), 0)
    qi = lax.broadcasted_iota(jnp.int32, (2 * WINDOW, WINDOW), 1)
    band = (si > qi) & (si <= qi + WINDOW)
    first_valid = jnp.where(n > 0, 0, WINDOW)
    lo = lax.broadcasted_iota(jnp.int32, (WINDOW, LANES), 1) < HEAD_DIM
    owns = [lo, jnp.logical_not(lo)]
    qs = q_ref[0] * (HEAD_DIM ** -0.5 * LOG2E)
    n_blk = ATTN_WIDTH // LANES
    group = n_blk // 2

    def scores(j):
        rows = slice(j * WINDOW, (j + 1) * WINDOW)
        keys = slice(j * WINDOW, (j + 2) * WINDOW)
        out = []
        for blk in range(n_blk):
            qb = qs[rows, blk * LANES:(blk + 1) * LANES]
            for hf in range(2):
                kk = ks[0 if blk // group == hf else 1][keys]
                out.append(lax.dot_general(kk, jnp.where(owns[hf], qb, 0.0).astype(BF16),
                                           (((1,), (1,)), ((), ())), preferred_element_type=F32))
        return out

    def finish(j, sc):
        rows = slice(j * WINDOW, (j + 1) * WINDOW)
        keys = slice(j * WINDOW, (j + 2) * WINDOW)
        mask = band & (si >= first_valid) if j == 0 else band
        outs = []
        for blk in range(n_blk):
            halves = []
            for hf in range(2):
                s = jnp.where(mask, sc[2 * blk + hf], -1e30)
                sink = sinks_ref[2 * blk + hf] * LOG2E
                m = jnp.maximum(jnp.max(s, axis=0, keepdims=True), sink)
                p = jnp.exp2(s - m).astype(BF16)
                vt = vts[0 if blk // group == hf else 1][hf][:, keys]
                pv = jnp.dot(vt, p, preferred_element_type=F32)
                num, psum = (pv[:HEAD_DIM], pv[HEAD_DIM:HEAD_DIM + 1]) if hf == 0 else (pv[HEAD_DIM:], pv[0:1])
                halves.append(num / (psum + jnp.exp2(sink - m)))
            outs.append(jnp.concatenate(halves, axis=0).T)
        g = g_ref[0, rows, :]
        o_ref[0, rows, :] = (jnp.concatenate(outs, axis=1) * (g * _sigmoid(g))).astype(o_ref.dtype)

    sc = scores(0)
    for j in range(ATTN_Q_BLOCKS):
        nxt = scores(j + 1) if j + 1 < ATTN_Q_BLOCKS else None
        finish(j, sc)
        sc = nxt


def _attn(sinks, q, kv, g_a):
    bsz, seq, _ = q.shape
    nb = seq // ATTN_ROWS
    return pl.pallas_call(
        _attn_kernel,
        grid=(bsz, nb),
        in_specs=[
            pl.BlockSpec(memory_space=pltpu.SMEM),
            pl.BlockSpec((1, ATTN_ROWS, ATTN_WIDTH), lambda b, n: (b, n, 0)),
            pl.BlockSpec((1, ATTN_ROWS, 2 * KV_WIDTH), lambda b, n: (b, n, 0)),
            pl.BlockSpec((1, WINDOW, 2 * KV_WIDTH), lambda b, n: (b, jnp.maximum(n * ATTN_Q_BLOCKS - 1, 0), 0)),
            pl.BlockSpec((1, ATTN_ROWS, ATTN_WIDTH), lambda b, n: (b, n, 0)),
        ],
        out_specs=pl.BlockSpec((1, ATTN_ROWS, ATTN_WIDTH), lambda b, n: (b, n, 0)),
        out_shape=jax.ShapeDtypeStruct((bsz, seq, ATTN_WIDTH), BF16),
        compiler_params=pltpu.CompilerParams(
            dimension_semantics=("arbitrary", "arbitrary"), vmem_limit_bytes=VMEM_LIMIT),
        name="swa_sink",
    )(sinks, q, kv, kv, g_a)


def _final_kernel(ya_ref, yb_ref, x_ref, w_ref, fg_ref, o_ref):
    for rows in _sub_rows(x_ref.shape[0]):
        xn = _out_residual(ya_ref, yb_ref, x_ref, w_ref, rows)
        ms = jnp.mean(xn * xn, axis=-1, keepdims=True)
        o_ref[rows, :] = xn * lax.rsqrt(ms + NORM_EPS) * fg_ref[...]


def _final(ya, yb, x2, w_bf16, fgain):
    rows = x2.shape[0]
    tm = ROW_TILE_FINAL
    row_spec = lambda w: pl.BlockSpec((tm, w), lambda i: (i, 0))
    return pl.pallas_call(
        _final_kernel,
        grid=(rows // tm,),
        in_specs=[row_spec(RWKV_WIDTH), row_spec(ATTN_WIDTH), row_spec(D_MODEL),
                  _resident((D_MODEL, D_MODEL)), _resident((1, D_MODEL))],
        out_specs=row_spec(D_MODEL),
        out_shape=jax.ShapeDtypeStruct((rows, D_MODEL), F32),
        compiler_params=pltpu.CompilerParams(
            dimension_semantics=("arbitrary",), vmem_limit_bytes=VMEM_LIMIT),
        name="outproj_final",
    )(ya, yb, x2, w_bf16, fgain)


def kernel(x, norm_gain, w_in, shift_mu, w0, w_up, a0, a_up, k_k, k_a, r_k, gn_w, gn_b, sinks, w_out, final_gain):
    bsz, seq, _ = x.shape
    depth = w_in.shape[0]
    rows = bsz * seq

    head_id = np.arange(LANES) // HEAD_DIM
    ones_pair = jnp.asarray(head_id[:, None] == head_id[None, :], dtype=BF16)
    tri = jnp.asarray(np.tril(np.ones((CHUNK, CHUNK), np.float32)), dtype=BF16)
    zpad = jnp.zeros((LORA, RWKV_WIDTH), F32)
    w_in_b = w_in.astype(BF16)
    w_out_b = w_out.astype(BF16)

    x2 = x.reshape(rows, D_MODEL)
    proj = _inproj(x2, norm_gain[0][None, :], w_in_b[0])
    for l in range(depth):
        ps, g_r, q, kv, g_a = proj
        wup_pad = jnp.concatenate([w_up[l], zpad], axis=0).astype(BF16)
        aup_pad = jnp.concatenate([zpad, a_up[l]], axis=0).astype(BF16)
        y_a = _rwkv(ps.reshape(bsz, seq, SHIFT_WIDTH), g_r.reshape(bsz, seq, RWKV_WIDTH),
                    shift_mu[l][None, :], w0[l][None, :], wup_pad, a0[l][None, :], aup_pad,
                    k_k[l][None, :], k_a[l][None, :], r_k[l].reshape(1, RWKV_WIDTH),
                    gn_w[l][None, :], gn_b[l][None, :], ones_pair, tri)
        y_b = _attn(sinks[l], q.reshape(bsz, seq, ATTN_WIDTH), kv.reshape(bsz, seq, 2 * KV_WIDTH),
                    g_a.reshape(bsz, seq, ATTN_WIDTH))
        y_a = y_a.reshape(rows, RWKV_WIDTH)
        y_b = y_b.reshape(rows, ATTN_WIDTH)
        if l + 1 < depth:
            x2, *proj = _mid(y_a, y_b, x2, w_out_b[l], norm_gain[l + 1][None, :], w_in_b[l + 1])
        else:
            x2 = _final(y_a, y_b, x2, w_out_b[l], final_gain[None, :])
    return x2.reshape(bsz, seq, D_MODEL)
```

```python
import functools

import numpy as np
import jax
import jax.numpy as jnp
from jax import lax
from jax.experimental import pallas as pl
from jax.experimental.pallas import tpu as pltpu

D_MODEL = 1024
HEAD_DIM = 64
RWKV_WIDTH = 512
ATTN_WIDTH = 512
KV_WIDTH = 128
LORA = 64
SHIFT_WIDTH = 3 * RWKV_WIDTH + 2 * LORA
IN_WIDTH = SHIFT_WIDTH + RWKV_WIDTH + ATTN_WIDTH + 2 * KV_WIDTH + ATTN_WIDTH
NORM_EPS = 1e-5
GN_EPS = 64e-5
WINDOW = 128
CHUNK = 128
LANES = 128
PAIRS = RWKV_WIDTH // LANES
CHUNKS_PER_STEP = 2
STEP_ROWS = CHUNKS_PER_STEP * CHUNK
ATTN_Q_BLOCKS = 8
ATTN_ROWS = ATTN_Q_BLOCKS * WINDOW
LOG2E = 1.4426950408889634
ROW_TILE = 512
ROW_TILE_FINAL = 1024
SUB_ROWS = 256
VMEM_LIMIT = 48 * 1024 * 1024

F32 = jnp.float32
BF16 = jnp.bfloat16


def _mm(a, b):
    return jnp.dot(a.astype(BF16), b.astype(BF16), preferred_element_type=F32)


def _sigmoid(x):
    return 0.5 * jnp.tanh(0.5 * x) + 0.5


PROJ_WIDTHS = (SHIFT_WIDTH, RWKV_WIDTH, ATTN_WIDTH, 2 * KV_WIDTH, ATTN_WIDTH)


def _sub_rows(tm):
    return [slice(r0, r0 + SUB_ROWS) for r0 in range(0, tm, SUB_ROWS)]


def _norm_project(xs, gain_ref, w_ref, out_refs):
    hs = []
    for x in xs:
        ms = jnp.mean(x * x, axis=-1, keepdims=True)
        hs.append((x * lax.rsqrt(ms + NORM_EPS) * gain_ref[...]).astype(BF16))
    for h, rows in zip(hs, _sub_rows(out_refs[0].shape[0])):
        c0 = 0
        for ref in out_refs:
            w = ref.shape[-1]
            ref[rows, :] = jnp.dot(h, w_ref[:, c0:c0 + w], preferred_element_type=F32)
            c0 += w


def _inproj_kernel(x_ref, gain_ref, w_ref, *out_refs):
    _norm_project([x_ref[rows, :] for rows in _sub_rows(x_ref.shape[0])], gain_ref, w_ref, out_refs)


def _resident(shape):
    return pl.BlockSpec(shape, lambda i: (0,) * len(shape), pipeline_mode=pl.Buffered(1))


def _inproj(x2, gain, w_bf16):
    rows = x2.shape[0]
    tm = ROW_TILE
    return pl.pallas_call(
        _inproj_kernel,
        grid=(rows // tm,),
        in_specs=[
            pl.BlockSpec((tm, D_MODEL), lambda i: (i, 0)),
            _resident((1, D_MODEL)),
            _resident((D_MODEL, IN_WIDTH)),
        ],
        out_specs=[pl.BlockSpec((tm, w), lambda i: (i, 0)) for w in PROJ_WIDTHS],
        out_shape=[jax.ShapeDtypeStruct((rows, w), F32) for w in PROJ_WIDTHS],
        compiler_params=pltpu.CompilerParams(
            dimension_semantics=("arbitrary",), vmem_limit_bytes=VMEM_LIMIT),
        name="inproj",
    )(x2, gain, w_bf16)


def _out_residual(ya_ref, yb_ref, x_ref, w_ref, rows):
    y = (jnp.dot(ya_ref[rows, :], w_ref[:RWKV_WIDTH, :], preferred_element_type=F32)
         + jnp.dot(yb_ref[rows, :], w_ref[RWKV_WIDTH:, :], preferred_element_type=F32))
    return x_ref[rows, :] + y


def _mid_kernel(ya_ref, yb_ref, x_ref, wo_ref, gain_ref, wi_ref, xo_ref, *out_refs):
    xs = []
    for rows in _sub_rows(x_ref.shape[0]):
        xn = _out_residual(ya_ref, yb_ref, x_ref, wo_ref, rows)
        xo_ref[rows, :] = xn
        xs.append(xn)
    _norm_project(xs, gain_ref, wi_ref, out_refs)


def _mid(ya, yb, x2, wo_bf16, gain, wi_bf16):
    rows = x2.shape[0]
    tm = ROW_TILE
    row_spec = lambda w: pl.BlockSpec((tm, w), lambda i: (i, 0))
    return pl.pallas_call(
        _mid_kernel,
        grid=(rows // tm,),
        in_specs=[row_spec(RWKV_WIDTH), row_spec(ATTN_WIDTH), row_spec(D_MODEL),
                  _resident((D_MODEL, D_MODEL)), _resident((1, D_MODEL)), _resident((D_MODEL, IN_WIDTH))],
        out_specs=[row_spec(D_MODEL)] + [row_spec(w) for w in PROJ_WIDTHS],
        out_shape=[jax.ShapeDtypeStruct((rows, D_MODEL), F32)]
        + [jax.ShapeDtypeStruct((rows, w), F32) for w in PROJ_WIDTHS],
        compiler_params=pltpu.CompilerParams(
            dimension_semantics=("arbitrary",), vmem_limit_bytes=VMEM_LIMIT),
        name="outproj_inproj",
    )(ya, yb, x2, wo_bf16, gain, wi_bf16)


def _chunk_scratch():
    return [
        pltpu.VMEM((2 * CHUNK, RWKV_WIDTH), BF16),
        pltpu.VMEM((RWKV_WIDTH, 2 * CHUNK), BF16),
        pltpu.VMEM((2 * RWKV_WIDTH, CHUNK), BF16),
        pltpu.VMEM((CHUNK, RWKV_WIDTH), BF16),
        pltpu.VMEM((2, CHUNK, RWKV_WIDTH), BF16),
        pltpu.VMEM((CHUNK, RWKV_WIDTH), F32),
        pltpu.VMEM((CHUNK, RWKV_WIDTH), F32),
        pltpu.VMEM((CHUNK, RWKV_WIDTH), F32),
        pltpu.VMEM((8, RWKV_WIDTH), F32),
    ]


N_CHUNK_SCRATCH = len(_chunk_scratch())


def _token_shift(ps_ref, rows, cols, prev_last, mu_ref, anchor):
    first_row = lax.broadcasted_iota(jnp.int32, (CHUNK, 1), 0) == 0
    p = ps_ref[0, rows, cols]
    prev = jnp.where(first_row, prev_last[:, cols], pltpu.roll(p, 1, 0))
    return p + (prev - p) * (mu_ref[:, cols] + anchor)


def _frontend(ps_ref, ck, prev_last, lw, la, params, ones_pair, tri, scr, anchors):
    mu_ref, w0_ref, a0_ref, kkw_ref, ka_ref, rk_ref = params
    lhs_ref, rhs_ref, ht_ref, vsw_ref, vso_ref, as_ref, rs_ref, bonus_ref, gam_ref = scr
    rows = slice(ck * CHUNK, (ck + 1) * CHUNK)
    lo = lax.broadcasted_iota(jnp.int32, (CHUNK, LANES), 1) < HEAD_DIM

    def shifted(cols, anchor):
        return _token_shift(ps_ref, rows, cols, prev_last, mu_ref, anchor)

    for pr in range(PAIRS):
        sl = slice(pr * LANES, (pr + 1) * LANES)
        anchor = anchors[-1]
        r = shifted(slice(pr * LANES, (pr + 1) * LANES), anchor)
        k = shifted(slice(RWKV_WIDTH + pr * LANES, RWKV_WIDTH + (pr + 1) * LANES), anchor)
        v = shifted(slice(2 * RWKV_WIDTH + pr * LANES, 2 * RWKV_WIDTH + (pr + 1) * LANES), anchor)
        kkraw = k * kkw_ref[:, sl]
        sq_b = (kkraw * kkraw).astype(BF16)
        ld = _sigmoid(w0_ref[:, sl] + lw[:, sl]) * float(-LOG2E * np.exp(-0.5))
        a = _sigmoid(a0_ref[:, sl] + la[:, sl])
        kp = k * (1.0 + (a - 1.0) * ka_ref[:, sl])
        rkp_b = (r * kp * rk_ref[:, sl]).astype(BF16)
        l_hi = ld.astype(BF16)
        l_lo = (ld - l_hi.astype(F32)).astype(BF16)
        yield
        sums = jnp.dot(jnp.concatenate([sq_b, rkp_b], axis=0), ones_pair, preferred_element_type=F32)
        cs = jnp.dot(tri, l_hi, preferred_element_type=F32) + jnp.dot(tri, l_lo, preferred_element_type=F32)
        yield
        bonus_ref[:, sl] = sums[CHUNK:] * v
        kk = kkraw * lax.rsqrt(jnp.maximum(sums[:CHUNK], 1e-24))
        bv = kk * a
        mid = cs[CHUNK // 2 - 1:CHUNK // 2, :]
        end = cs[CHUNK - 1:CHUNK, :]
        e_out = jnp.exp2(mid - cs)
        e_end = jnp.exp2(end - cs)
        a_t = -kk * jnp.exp2(cs - ld - mid)
        r_t = r * jnp.exp2(cs - mid)
        g_mid = jnp.exp2(mid)
        as_ref[:, sl] = a_t * g_mid
        rs_ref[:, sl] = r_t * g_mid
        gam_ref[:, sl] = jnp.broadcast_to(jnp.exp2(end), (gam_ref.shape[0], LANES))
        lhs_ref[:CHUNK, sl] = a_t.astype(BF16)
        lhs_ref[CHUNK:, sl] = r_t.astype(BF16)
        rhs_ref[sl, :CHUNK] = (bv * e_out).T.astype(BF16)
        rhs_ref[sl, CHUNK:] = (kp * e_out).T.astype(BF16)
        ht_ref[sl, :] = (bv * e_end).T.astype(BF16)
        ht_ref[RWKV_WIDTH + pr * LANES:RWKV_WIDTH + (pr + 1) * LANES, :] = (kp * e_end).T.astype(BF16)
        v_sw = pltpu.roll(v, HEAD_DIM, 1)
        vsw_ref[:, sl] = v_sw.astype(BF16)
        vso_ref[0, :, sl] = jnp.where(lo, 0.0, v_sw).astype(BF16)
        vso_ref[1, :, sl] = jnp.where(lo, v_sw, 0.0).astype(BF16)
        yield


def _odd_rows(x, sh):
    k = 1 << sh
    return jnp.concatenate([x[r0:r0 + k] for r0 in range(k, CHUNK, 2 * k)], axis=0)


def _spread_odd_rows(xc, sh):
    k = 1 << sh
    zero = jnp.zeros((k, xc.shape[1]), xc.dtype)
    pieces = []
    for i in range(CHUNK // (2 * k)):
        pieces += [zero, xc[i * k:(i + 1) * k]]
    return jnp.concatenate(pieces, axis=0)


def _zero_row(x):
    row = x[0:1, :]
    return jnp.where(row != row, 1.0, 0.0)


def _solver(g_ref, gnw_ref, gnb_ref, ones_pair, out_ref, st_ref, scrs, anchors):
    row = lax.broadcasted_iota(jnp.int32, (CHUNK, CHUNK), 0)
    col = lax.broadcasted_iota(jnp.int32, (CHUNK, CHUNK), 1)
    strict = col < row
    incl2 = jnp.concatenate([col <= row, col <= row], axis=1)
    lo = col < HEAD_DIM
    owns = [lo, jnp.logical_not(lo)]
    same_head = (row >> 6) == (col >> 6)
    eye = row == col
    blk8 = (row >> 3) == (col >> 3)
    n_ck = len(scrs)
    heads = [(ck, pr, hh) for ck in range(n_ck) for pr in range(PAIRS) for hh in range(2)]
    sls = [slice(pr * LANES, (pr + 1) * LANES) for pr in range(PAIRS)]
    lhs_refs, rhs_refs, ht_refs, vsw_refs, vso_refs, as_refs, rs_refs, bonus_refs, gam_refs = zip(*scrs)

    zrows = jnp.zeros((HEAD_DIM, 2 * CHUNK), BF16)

    def head_rows(ref, pr, hh):
        rows = ref[pr * LANES + hh * HEAD_DIM:pr * LANES + (hh + 1) * HEAD_DIM, :]
        return jnp.concatenate([rows, zrows] if hh == 0 else [zrows, rows], axis=0)

    amats = [jnp.dot(lhs_refs[ck][:, sls[pr]], head_rows(rhs_refs[ck], pr, hh), preferred_element_type=F32)
             for ck, pr, hh in heads]
    yield
    n_mats = [jnp.where(strict, am[:CHUNK, :CHUNK], 0.0) for am in amats]
    avs, kvs = [], []
    for ck in range(n_ck):
        for pr in range(PAIRS):
            i0 = (ck * PAIRS + pr) * 2
            stacked = jnp.concatenate(
                [jnp.where(strict, amats[i0 + hh][:CHUNK, CHUNK:], 0.0).astype(BF16) for hh in range(2)]
                + [ht_refs[ck][RWKV_WIDTH + pr * LANES:RWKV_WIDTH + (pr + 1) * LANES, :]], axis=0)
            out = jnp.dot(stacked, vsw_refs[ck][:, sls[pr]], preferred_element_type=F32)
            avs += [out[:CHUNK], out[CHUNK:2 * CHUNK]]
            kvs.append(out[2 * CHUNK:])
    nds = [jnp.where(blk8, n, 0.0) for n in n_mats]
    ndbs = [nd.astype(BF16) for nd in nds]
    s2s = [jnp.dot(ndb, ndb, preferred_element_type=F32) for ndb in ndbs]
    anchors.append(_zero_row(s2s[-1]))
    yield
    ps = [jnp.where(eye, 1.0, nd) for nd in nds]
    pss = [_mm(jnp.concatenate([p, s2], axis=0), s2) for p, s2 in zip(ps, s2s)]
    yield
    ps = [p + x[:CHUNK] for p, x in zip(ps, pss)]
    d8s = [p + _mm(p, x[CHUNK:]) for p, x in zip(ps, pss)]
    dbs = [d.astype(BF16) for d in d8s]
    yield
    for sh in (3, 4, 5):
        off = ((row >> (sh + 1)) == (col >> (sh + 1))) & ((row >> sh) != (col >> sh))
        ts = [jnp.dot(_odd_rows(jnp.where(off, n, 0.0), sh).astype(BF16), db, preferred_element_type=F32)
              for n, db in zip(n_mats, dbs)]
        if sh == 3:
            anchors.append(_zero_row(ts[-1]))
        yield
        d_odd = [_odd_rows(d8, sh).astype(BF16) for d8 in d8s] if sh == 3 else [_odd_rows(db, sh) for db in dbs]
        us = [jnp.dot(do, _spread_odd_rows(t, sh).astype(BF16), preferred_element_type=F32)
              for do, t in zip(d_odd, ts)]
        dbs = [db + _spread_odd_rows(u, sh).astype(BF16) for db, u in zip(dbs, us)]
        if sh == 4:
            anchors.append(_zero_row(us[-1]))
        yield
    half = CHUNK // 2
    y0s = [jnp.dot(db, jnp.where(owns[hh], as_refs[ck][:, sls[pr]], av).astype(BF16), preferred_element_type=F32)
           for (ck, pr, hh), db, av in zip(heads, dbs, avs)]
    anchors.append(_zero_row(y0s[-1]))
    yield
    y0bs = [y0.astype(BF16) for y0 in y0s]
    lo_half = lax.broadcasted_iota(jnp.int32, (half, CHUNK), 1) < half
    ws = [jnp.dot(jnp.where(lo_half, n[half:], 0.0).astype(BF16), y0b, preferred_element_type=F32)
          for n, y0b in zip(n_mats, y0bs)]
    anchors.append(_zero_row(ws[-1]))
    yield
    xss = [jnp.concatenate(
        [y0b[:half],
         (y0[half:] + jnp.dot(db[half:], jnp.concatenate([y0b[:half], w.astype(BF16)], axis=0),
                              preferred_element_type=F32)).astype(BF16)], axis=0)
        for y0, y0b, db, w in zip(y0s, y0bs, dbs, ws)]
    yield
    zpad = jnp.zeros((LANES, CHUNK), BF16)
    zrs = [jnp.dot(jnp.concatenate([jnp.where(incl2, am[CHUNK:, :], 0.0).astype(BF16),
                                    jnp.concatenate([ht_refs[ck][sls[pr], :], zpad], axis=1)], axis=0),
                   jnp.concatenate([xs, vso_refs[ck][hh, :, sls[pr]]], axis=0), preferred_element_type=F32)
           for (ck, pr, hh), am, xs in zip(heads, amats, xss)]
    zs = [zr[:CHUNK] for zr in zrs]
    res = [zr[CHUNK:] for zr in zrs]
    anchors.append(_zero_row(zrs[-1]))
    yield
    ys = [[] for _ in range(n_ck)]
    for pr in range(PAIRS):
        sl = sls[pr]
        st = st_ref[pr]
        for ck in range(n_ck):
            i0 = (ck * PAIRS + pr) * 2
            z0, z1 = zs[i0], zs[i0 + 1]
            cat = jnp.concatenate([res[i0][:HEAD_DIM], res[i0 + 1][HEAD_DIM:]], axis=0)
            m_t = jnp.where(same_head, cat, 0.0) + jnp.where(eye, gam_refs[ck][0:1, sl], 0.0)
            g_t = jnp.where(same_head, 0.0, cat + kvs[ck * PAIRS + pr])
            q_hat = rs_refs[ck][:, sl] + jnp.where(lo, z0, z1)
            y_loc_sw = jnp.where(lo, z1, z0)
            qm = _mm(jnp.concatenate([q_hat, m_t], axis=0), st)
            ys[ck].append(qm[:CHUNK] + y_loc_sw)
            st = qm[CHUNK:] + g_t
        st_ref[pr] = st
    yield
    y_all = jnp.concatenate([pltpu.roll(y_sw, HEAD_DIM, 1) for ck in range(n_ck) for y_sw in ys[ck]], axis=0)
    mean = _mm(y_all, ones_pair) * (1.0 / HEAD_DIM)
    dlt = y_all - mean
    var = _mm(dlt * dlt, ones_pair) * (1.0 / HEAD_DIM)
    yn_all = dlt * lax.rsqrt(var + GN_EPS)
    for ck in range(n_ck):
        rows = slice(ck * CHUNK, (ck + 1) * CHUNK)
        for pr in range(PAIRS):
            sl = sls[pr]
            blk = slice((ck * PAIRS + pr) * CHUNK, (ck * PAIRS + pr + 1) * CHUNK)
            g = g_ref[0, rows, sl]
            yn = yn_all[blk] * gnw_ref[:, sl] + gnb_ref[:, sl]
            out_ref[0, rows, sl] = ((yn + bonus_refs[ck][:, sl]) * (g * _sigmoid(g))).astype(out_ref.dtype)
    yield


_SOLVER_LEAD = 4
_ALTERNATIONS = 12


def _rwkv_kernel(ps_ref, prev_ref, g_ref, mu_ref, w0_ref, wup_ref, a0_ref, aup_ref,
                 kkw_ref, ka_ref, rk_ref, gnw_ref, gnb_ref, ones_ref, tri_ref,
                 out_ref, st_ref, *chunk_scr, last):
    s = pl.program_id(1)
    sets = [chunk_scr[i * N_CHUNK_SCRATCH:(i + 1) * N_CHUNK_SCRATCH] for i in range(2 * CHUNKS_PER_STEP)]
    slot_a, slot_b = sets[:CHUNKS_PER_STEP], sets[CHUNKS_PER_STEP:]
    params = (mu_ref, w0_ref, a0_ref, kkw_ref, ka_ref, rk_ref)

    def step(fe_scrs, sv_scrs):
        anchors = [0.0]
        svs = []
        if sv_scrs is not None:
            svs.append(_solver(g_ref, gnw_ref, gnb_ref, ones_ref[...], out_ref, st_ref, sv_scrs, anchors))
        fes = []
        if fe_scrs is not None:
            prev_lasts = [prev_ref[0, 7:8, :] * jnp.where(s > 0, 1.0, 0.0)]
            prev_lasts += [ps_ref[0, ck * CHUNK - 1:ck * CHUNK, :] for ck in range(1, CHUNKS_PER_STEP)]
            wa = jnp.concatenate(
                [_token_shift(ps_ref, slice(ck * CHUNK, (ck + 1) * CHUNK), slice(3 * RWKV_WIDTH, SHIFT_WIDTH),
                              prev_lasts[ck], mu_ref, 0.0) for ck in range(CHUNKS_PER_STEP)], axis=0)
            wa_b = wa.astype(BF16)
            wt_b = jnp.tanh(wa).astype(BF16)
        for gen in svs * _SOLVER_LEAD:
            next(gen)
        if fe_scrs is not None:
            lw = jnp.dot(wt_b, wup_ref[...], preferred_element_type=F32)
            la = jnp.dot(wa_b, aup_ref[...], preferred_element_type=F32)
            for ck in range(CHUNKS_PER_STEP):
                rows = slice(ck * CHUNK, (ck + 1) * CHUNK)
                fes.append(_frontend(ps_ref, ck, prev_lasts[ck], lw[rows], la[rows], params, ones_ref[...],
                                     tri_ref[...], fe_scrs[ck], anchors))
        for _ in range(_ALTERNATIONS):
            for gen in fes + svs:
                next(gen, None)
        for gen in svs + fes:
            for _ in gen:
                pass

    @pl.when(s == 0)
    def _():
        st_ref[...] = jnp.zeros_like(st_ref)
        step(slot_a, None)

    slots = [slot_a, slot_b]
    for parity in range(2):
        @pl.when((s > 0) & (s < last) & ((s & 1) == parity))
        def _():
            step(slots[parity], slots[1 - parity])

    @pl.when(s == last)
    def _():
        step(None, slots[1 - last % 2])


def _rwkv(ps, g_r, mu, w0, wup_pad, a0, aup_pad, kkw, ka, rk, gnw, gnb, ones_pair, tri):
    bsz, seq, _ = ps.shape
    ns = seq // STEP_ROWS
    sub = STEP_ROWS // 8

    def vec(width):
        return pl.BlockSpec((1, width), lambda b, s: (0, 0))

    def fe_block(s):
        return jnp.minimum(s, ns - 1)

    def sv_block(s):
        return jnp.maximum(s - 1, 0)

    return pl.pallas_call(
        functools.partial(_rwkv_kernel, last=ns),
        grid=(bsz, ns + 1),
        in_specs=[
            pl.BlockSpec((1, STEP_ROWS, SHIFT_WIDTH), lambda b, s: (b, fe_block(s), 0)),
            pl.BlockSpec((1, 8, SHIFT_WIDTH), lambda b, s: (b, jnp.maximum(fe_block(s) * sub - 1, 0), 0)),
            pl.BlockSpec((1, STEP_ROWS, RWKV_WIDTH), lambda b, s: (b, sv_block(s), 0)),
            vec(SHIFT_WIDTH),
            vec(RWKV_WIDTH),
            pl.BlockSpec((LANES, RWKV_WIDTH), lambda b, s: (0, 0)),
            vec(RWKV_WIDTH),
            pl.BlockSpec((LANES, RWKV_WIDTH), lambda b, s: (0, 0)),
            vec(RWKV_WIDTH), vec(RWKV_WIDTH), vec(RWKV_WIDTH), vec(RWKV_WIDTH), vec(RWKV_WIDTH),
            pl.BlockSpec((LANES, LANES), lambda b, s: (0, 0)),
            pl.BlockSpec((CHUNK, CHUNK), lambda b, s: (0, 0)),
        ],
        out_specs=pl.BlockSpec((1, STEP_ROWS, RWKV_WIDTH), lambda b, s: (b, sv_block(s), 0)),
        out_shape=jax.ShapeDtypeStruct((bsz, seq, RWKV_WIDTH), BF16),
        scratch_shapes=[pltpu.VMEM((PAIRS, LANES, LANES), F32)]
        + [buf for _ in range(2 * CHUNKS_PER_STEP) for buf in _chunk_scratch()],
        compiler_params=pltpu.CompilerParams(
            dimension_semantics=("arbitrary", "arbitrary"), vmem_limit_bytes=VMEM_LIMIT),
        name="rwkv7_chunked",
    )(ps, ps, g_r, mu, w0, wup_pad, a0, aup_pad, kkw, ka, rk, gnw, gnb, ones_pair, tri)


def _attn_kernel(sinks_ref, q_ref, kvc_ref, kvp_ref, g_ref, o_ref):
    n = pl.program_id(1)
    kv = jnp.concatenate([kvp_ref[0], kvc_ref[0]], axis=0)
    k_nat = kv[:, :KV_WIDTH]
    v_nat = kv[:, KV_WIDTH:]
    ks = [k_nat.astype(BF16), pltpu.roll(k_nat, HEAD_DIM, 1).astype(BF16)]
    vts = [v_nat.T.astype(BF16), pltpu.roll(v_nat, HEAD_DIM, 1).T.astype(BF16)]
    ones_rows = jnp.ones((HEAD_DIM, vts[0].shape[1]), BF16)
    vts = [[jnp.concatenate([vt[:HEAD_DIM], ones_rows], axis=0), jnp.concatenate([ones_rows, vt[HEAD_DIM:]], axis=0)]
           for vt in vts]
    si = lax.broadcasted_iota(jnp.int32, (2 * WINDOW, WINDOW), 0)
    qi = lax.broadcasted_iota(jnp.int32, (2 * WINDOW, WINDOW), 1)
    band = (si > qi) & (si <= qi + WINDOW)
    first_valid = jnp.where(n > 0, 0, WINDOW)
    lo = lax.broadcasted_iota(jnp.int32, (WINDOW, LANES), 1) < HEAD_DIM
    owns = [lo, jnp.logical_not(lo)]
    qs = q_ref[0] * (HEAD_DIM ** -0.5 * LOG2E)
    n_blk = ATTN_WIDTH // LANES
    group = n_blk // 2

    def scores(j):
        rows = slice(j * WINDOW, (j + 1) * WINDOW)
        keys = slice(j * WINDOW, (j + 2) * WINDOW)
        out = []
        for blk in range(n_blk):
            qb = qs[rows, blk * LANES:(blk + 1) * LANES]
            for hf in range(2):
                kk = ks[0 if blk // group == hf else 1][keys]
                out.append(lax.dot_general(kk, jnp.where(owns[hf], qb, 0.0).astype(BF16),
                                           (((1,), (1,)), ((), ())), preferred_element_type=F32))
        return out

    def finish(j, sc):
        rows = slice(j * WINDOW, (j + 1) * WINDOW)
        keys = slice(j * WINDOW, (j + 2) * WINDOW)
        mask = band & (si >= first_valid) if j == 0 else band
        outs = []
        for blk in range(n_blk):
            halves = []
            for hf in range(2):
                s = jnp.where(mask, sc[2 * blk + hf], -1e30)
                sink = sinks_ref[2 * blk + hf] * LOG2E
                m = jnp.maximum(jnp.max(s, axis=0, keepdims=True), sink)
                p = jnp.exp2(s - m).astype(BF16)
                vt = vts[0 if blk // group == hf else 1][hf][:, keys]
                pv = jnp.dot(vt, p, preferred_element_type=F32)
                num, psum = (pv[:HEAD_DIM], pv[HEAD_DIM:HEAD_DIM + 1]) if hf == 0 else (pv[HEAD_DIM:], pv[0:1])
                halves.append(num / (psum + jnp.exp2(sink - m)))
            outs.append(jnp.concatenate(halves, axis=0).T)
        g = g_ref[0, rows, :]
        o_ref[0, rows, :] = (jnp.concatenate(outs, axis=1) * (g * _sigmoid(g))).astype(o_ref.dtype)

    sc = scores(0)
    for j in range(ATTN_Q_BLOCKS):
        nxt = scores(j + 1) if j + 1 < ATTN_Q_BLOCKS else None
        finish(j, sc)
        sc = nxt


def _attn(sinks, q, kv, g_a):
    bsz, seq, _ = q.shape
    nb = seq // ATTN_ROWS
    return pl.pallas_call(
        _attn_kernel,
        grid=(bsz, nb),
        in_specs=[
            pl.BlockSpec(memory_space=pltpu.SMEM),
            pl.BlockSpec((1, ATTN_ROWS, ATTN_WIDTH), lambda b, n: (b, n, 0)),
            pl.BlockSpec((1, ATTN_ROWS, 2 * KV_WIDTH), lambda b, n: (b, n, 0)),
            pl.BlockSpec((1, WINDOW, 2 * KV_WIDTH), lambda b, n: (b, jnp.maximum(n * ATTN_Q_BLOCKS - 1, 0), 0)),
            pl.BlockSpec((1, ATTN_ROWS, ATTN_WIDTH), lambda b, n: (b, n, 0)),
        ],
        out_specs=pl.BlockSpec((1, ATTN_ROWS, ATTN_WIDTH), lambda b, n: (b, n, 0)),
        out_shape=jax.ShapeDtypeStruct((bsz, seq, ATTN_WIDTH), BF16),
        compiler_params=pltpu.CompilerParams(
            dimension_semantics=("arbitrary", "arbitrary"), vmem_limit_bytes=VMEM_LIMIT),
        name="swa_sink",
    )(sinks, q, kv, kv, g_a)


def _final_kernel(ya_ref, yb_ref, x_ref, w_ref, fg_ref, o_ref):
    for rows in _sub_rows(x_ref.shape[0]):
        xn = _out_residual(ya_ref, yb_ref, x_ref, w_ref, rows)
        ms = jnp.mean(xn * xn, axis=-1, keepdims=True)
        o_ref[rows, :] = xn * lax.rsqrt(ms + NORM_EPS) * fg_ref[...]


def _final(ya, yb, x2, w_bf16, fgain):
    rows = x2.shape[0]
    tm = ROW_TILE_FINAL
    row_spec = lambda w: pl.BlockSpec((tm, w), lambda i: (i, 0))
    return pl.pallas_call(
        _final_kernel,
        grid=(rows // tm,),
        in_specs=[row_spec(RWKV_WIDTH), row_spec(ATTN_WIDTH), row_spec(D_MODEL),
                  _resident((D_MODEL, D_MODEL)), _resident((1, D_MODEL))],
        out_specs=row_spec(D_MODEL),
        out_shape=jax.ShapeDtypeStruct((rows, D_MODEL), F32),
        compiler_params=pltpu.CompilerParams(
            dimension_semantics=("arbitrary",), vmem_limit_bytes=VMEM_LIMIT),
        name="outproj_final",
    )(ya, yb, x2, w_bf16, fgain)


def kernel(x, norm_gain, w_in, shift_mu, w0, w_up, a0, a_up, k_k, k_a, r_k, gn_w, gn_b, sinks, w_out, final_gain):
    bsz, seq, d_model = x.shape
    depth = w_in.shape[0]
    rows = bsz * seq
    assert d_model == D_MODEL and w_in.shape[1:] == (D_MODEL, IN_WIDTH) and w_out.shape[1:] == (D_MODEL, D_MODEL)
    assert seq % ATTN_ROWS == 0 and seq % STEP_ROWS == 0 and seq % ROW_TILE == 0 and rows % ROW_TILE_FINAL == 0

    head_id = np.arange(LANES) // HEAD_DIM
    ones_pair = jnp.asarray(head_id[:, None] == head_id[None, :], dtype=BF16)
    tri = jnp.asarray(np.tril(np.ones((CHUNK, CHUNK), np.float32)), dtype=BF16)
    zpad = jnp.zeros((LORA, RWKV_WIDTH), F32)
    w_in_b = w_in.astype(BF16)
    w_out_b = w_out.astype(BF16)

    x2 = x.reshape(rows, D_MODEL)
    proj = _inproj(x2, norm_gain[0][None, :], w_in_b[0])
    for l in range(depth):
        ps, g_r, q, kv, g_a = proj
        wup_pad = jnp.concatenate([w_up[l], zpad], axis=0).astype(BF16)
        aup_pad = jnp.concatenate([zpad, a_up[l]], axis=0).astype(BF16)
        y_a = _rwkv(ps.reshape(bsz, seq, SHIFT_WIDTH), g_r.reshape(bsz, seq, RWKV_WIDTH),
                    shift_mu[l][None, :], w0[l][None, :], wup_pad, a0[l][None, :], aup_pad,
                    k_k[l][None, :], k_a[l][None, :], r_k[l].reshape(1, RWKV_WIDTH),
                    gn_w[l][None, :], gn_b[l][None, :], ones_pair, tri)
        y_b = _attn(sinks[l], q.reshape(bsz, seq, ATTN_WIDTH), kv.reshape(bsz, seq, 2 * KV_WIDTH),
                    g_a.reshape(bsz, seq, ATTN_WIDTH))
        y_a = y_a.reshape(rows, RWKV_WIDTH)
        y_b = y_b.reshape(rows, ATTN_WIDTH)
        if l + 1 < depth:
            x2, *proj = _mid(y_a, y_b, x2, w_out_b[l], norm_gain[l + 1][None, :], w_in_b[l + 1])
        else:
            x2 = _final(y_a, y_b, x2, w_out_b[l], final_gain[None, :])
    return x2.reshape(bsz, seq, D_MODEL)
```

```python
import functools

import numpy as np
import jax
import jax.numpy as jnp
from jax import lax
from jax.experimental import pallas as pl
from jax.experimental.pallas import tpu as pltpu

D_MODEL = 1024
HEAD_DIM = 64
RWKV_WIDTH = 512
ATTN_WIDTH = 512
KV_WIDTH = 128
LORA = 64
SHIFT_WIDTH = 3 * RWKV_WIDTH + 2 * LORA
IN_WIDTH = SHIFT_WIDTH + RWKV_WIDTH + ATTN_WIDTH + 2 * KV_WIDTH + ATTN_WIDTH
NORM_EPS = 1e-5
GN_EPS = 64e-5
WINDOW = 128
CHUNK = 128
LANES = 128
PAIRS = RWKV_WIDTH // LANES
CHUNKS_PER_STEP = 2
STEP_ROWS = CHUNKS_PER_STEP * CHUNK
ATTN_Q_BLOCKS = 8
ATTN_ROWS = ATTN_Q_BLOCKS * WINDOW
LOG2E = 1.4426950408889634
ROW_TILE = 512
ROW_TILE_FINAL = 1024
SUB_ROWS = 256
VMEM_LIMIT = 48 * 1024 * 1024

F32 = jnp.float32
BF16 = jnp.bfloat16


def _mm(a, b):
    return jnp.dot(a.astype(BF16), b.astype(BF16), preferred_element_type=F32)


def _sigmoid(x):
    return 0.5 * jnp.tanh(0.5 * x) + 0.5


PROJ_WIDTHS = (SHIFT_WIDTH, RWKV_WIDTH, ATTN_WIDTH, 2 * KV_WIDTH, ATTN_WIDTH)


def _sub_rows(tm):
    return [slice(r0, r0 + SUB_ROWS) for r0 in range(0, tm, SUB_ROWS)]


def _norm_project(xs, gain_ref, w_ref, out_refs):
    hs = []
    for x in xs:
        ms = jnp.mean(x * x, axis=-1, keepdims=True)
        hs.append((x * lax.rsqrt(ms + NORM_EPS) * gain_ref[...]).astype(BF16))
    for h, rows in zip(hs, _sub_rows(out_refs[0].shape[0])):
        c0 = 0
        for ref in out_refs:
            w = ref.shape[-1]
            ref[rows, :] = jnp.dot(h, w_ref[:, c0:c0 + w], preferred_element_type=F32)
            c0 += w


def _inproj_kernel(x_ref, gain_ref, w_ref, *out_refs):
    _norm_project([x_ref[rows, :] for rows in _sub_rows(x_ref.shape[0])], gain_ref, w_ref, out_refs)


def _resident(shape):
    return pl.BlockSpec(shape, lambda i: (0,) * len(shape), pipeline_mode=pl.Buffered(1))


def _inproj(x2, gain, w_bf16):
    rows = x2.shape[0]
    tm = ROW_TILE
    return pl.pallas_call(
        _inproj_kernel,
        grid=(rows // tm,),
        in_specs=[
            pl.BlockSpec((tm, D_MODEL), lambda i: (i, 0)),
            _resident((1, D_MODEL)),
            _resident((D_MODEL, IN_WIDTH)),
        ],
        out_specs=[pl.BlockSpec((tm, w), lambda i: (i, 0)) for w in PROJ_WIDTHS],
        out_shape=[jax.ShapeDtypeStruct((rows, w), F32) for w in PROJ_WIDTHS],
        compiler_params=pltpu.CompilerParams(
            dimension_semantics=("arbitrary",), vmem_limit_bytes=VMEM_LIMIT),
        name="inproj",
    )(x2, gain, w_bf16)


def _out_residual(ya_ref, yb_ref, x_ref, w_ref, rows):
    y = (jnp.dot(ya_ref[rows, :], w_ref[:RWKV_WIDTH, :], preferred_element_type=F32)
         + jnp.dot(yb_ref[rows, :], w_ref[RWKV_WIDTH:, :], preferred_element_type=F32))
    return x_ref[rows, :] + y


def _mid_kernel(ya_ref, yb_ref, x_ref, wo_ref, gain_ref, wi_ref, xo_ref, *out_refs):
    xs = []
    for rows in _sub_rows(x_ref.shape[0]):
        xn = _out_residual(ya_ref, yb_ref, x_ref, wo_ref, rows)
        xo_ref[rows, :] = xn
        xs.append(xn)
    _norm_project(xs, gain_ref, wi_ref, out_refs)


def _mid(ya, yb, x2, wo_bf16, gain, wi_bf16):
    rows = x2.shape[0]
    tm = ROW_TILE
    row_spec = lambda w: pl.BlockSpec((tm, w), lambda i: (i, 0))
    return pl.pallas_call(
        _mid_kernel,
        grid=(rows // tm,),
        in_specs=[row_spec(RWKV_WIDTH), row_spec(ATTN_WIDTH), row_spec(D_MODEL),
                  _resident((D_MODEL, D_MODEL)), _resident((1, D_MODEL)), _resident((D_MODEL, IN_WIDTH))],
        out_specs=[row_spec(D_MODEL)] + [row_spec(w) for w in PROJ_WIDTHS],
        out_shape=[jax.ShapeDtypeStruct((rows, D_MODEL), F32)]
        + [jax.ShapeDtypeStruct((rows, w), F32) for w in PROJ_WIDTHS],
        compiler_params=pltpu.CompilerParams(
            dimension_semantics=("arbitrary",), vmem_limit_bytes=VMEM_LIMIT),
        name="outproj_inproj",
    )(ya, yb, x2, wo_bf16, gain, wi_bf16)


def _chunk_scratch():
    return [
        pltpu.VMEM((2 * CHUNK, RWKV_WIDTH), BF16),
        pltpu.VMEM((RWKV_WIDTH, 2 * CHUNK), BF16),
        pltpu.VMEM((2 * RWKV_WIDTH, CHUNK), BF16),
        pltpu.VMEM((CHUNK, RWKV_WIDTH), BF16),
        pltpu.VMEM((2, CHUNK, RWKV_WIDTH), BF16),
        pltpu.VMEM((CHUNK, RWKV_WIDTH), F32),
        pltpu.VMEM((CHUNK, RWKV_WIDTH), F32),
        pltpu.VMEM((CHUNK, RWKV_WIDTH), F32),
        pltpu.VMEM((8, RWKV_WIDTH), F32),
    ]


N_CHUNK_SCRATCH = len(_chunk_scratch())


def _token_shift(ps_ref, rows, cols, prev_last, mu_ref):
    first_row = lax.broadcasted_iota(jnp.int32, (CHUNK, 1), 0) == 0
    p = ps_ref[0, rows, cols]
    prev = jnp.where(first_row, prev_last[:, cols], pltpu.roll(p, 1, 0))
    return p + (prev - p) * mu_ref[:, cols]


def _frontend(ps_ref, ck, prev_last, lw, la, params, ones_pair, tri, scr, done, clock):
    mu_ref, w0_ref, a0_ref, kkw_ref, ka_ref, rk_ref = params
    lhs_ref, rhs_ref, ht_ref, vsw_ref, vso_ref, as_ref, rs_ref, bonus_ref, gam_ref = scr
    rows = slice(ck * CHUNK, (ck + 1) * CHUNK)
    lo = lax.broadcasted_iota(jnp.int32, (CHUNK, LANES), 1) < HEAD_DIM

    def shifted(cols):
        return _token_shift(ps_ref, rows, cols, prev_last, mu_ref)

    pre = []
    for pr in range(PAIRS):
        sl = slice(pr * LANES, (pr + 1) * LANES)
        r = shifted(slice(pr * LANES, (pr + 1) * LANES))
        k = shifted(slice(RWKV_WIDTH + pr * LANES, RWKV_WIDTH + (pr + 1) * LANES))
        v = shifted(slice(2 * RWKV_WIDTH + pr * LANES, 2 * RWKV_WIDTH + (pr + 1) * LANES))
        kkraw = k * kkw_ref[:, sl]
        sq_b = (kkraw * kkraw).astype(BF16)
        ld = _sigmoid(w0_ref[:, sl] + lw[:, sl]) * float(-LOG2E * np.exp(-0.5))
        a = _sigmoid(a0_ref[:, sl] + la[:, sl])
        kp = k * (1.0 + (a - 1.0) * ka_ref[:, sl])
        rkp_b = (r * kp * rk_ref[:, sl]).astype(BF16)
        l_hi = ld.astype(BF16)
        l_lo = (ld - l_hi.astype(F32)).astype(BF16)
        pre.append((r, v, kkraw, ld, a, kp, sq_b, rkp_b, l_hi, l_lo))

    def issue(pr):
        sq_b, rkp_b, l_hi, l_lo = pre[pr][6:]
        sums = jnp.dot(jnp.concatenate([sq_b, rkp_b], axis=0), ones_pair, preferred_element_type=F32)
        cs = jnp.dot(tri, l_hi, preferred_element_type=F32) + jnp.dot(tri, l_lo, preferred_element_type=F32)
        return sums, cs

    def finish(pr, sums, cs):
        sl = slice(pr * LANES, (pr + 1) * LANES)
        r, v, kkraw, ld, a, kp = pre[pr][:6]
        bonus_ref[:, sl] = sums[CHUNK:] * v
        kk = kkraw * lax.rsqrt(jnp.maximum(sums[:CHUNK], 1e-24))
        bv = kk * a
        mid = cs[CHUNK // 2 - 1:CHUNK // 2, :]
        end = cs[CHUNK - 1:CHUNK, :]
        e_out = jnp.exp2(mid - cs)
        e_end = jnp.exp2(end - cs)
        a_t = -kk * jnp.exp2(cs - ld - mid)
        r_t = r * jnp.exp2(cs - mid)
        g_mid = jnp.exp2(mid)
        a_s = a_t * g_mid
        r_s = r_t * g_mid
        as_ref[:, sl] = a_s
        rs_ref[:, sl] = r_s
        gam_ref[:, sl] = jnp.broadcast_to(jnp.exp2(end), (gam_ref.shape[0], LANES))
        lhs_ref[:CHUNK, sl] = a_t.astype(BF16)
        lhs_ref[CHUNK:, sl] = r_t.astype(BF16)
        b_out, k_out, b_end, k_end = (bv * e_out).T, (kp * e_out).T, (bv * e_end).T, (kp * e_end).T
        rhs_ref[sl, :CHUNK] = b_out.astype(BF16)
        rhs_ref[sl, CHUNK:] = k_out.astype(BF16)
        ht_ref[sl, :] = b_end.astype(BF16)
        ht_ref[RWKV_WIDTH + pr * LANES:RWKV_WIDTH + (pr + 1) * LANES, :] = k_end.astype(BF16)
        v_sw = pltpu.roll(v, HEAD_DIM, 1)
        vsw_ref[:, sl] = v_sw.astype(BF16)
        vso_ref[0, :, sl] = jnp.where(lo, 0.0, v_sw).astype(BF16)
        vso_ref[1, :, sl] = jnp.where(lo, v_sw, 0.0).astype(BF16)
        done.append((clock[0], sum(_zero_row(x) for x in (a_s, r_s, b_out, k_out, b_end, k_end, v_sw))))

    issued = {}
    for slot in _FE_SLOTS:
        for item in slot.split():
            pr = int(item[1])
            if item[0] == "d":
                issued[pr] = issue(pr)
            else:
                finish(pr, *issued.pop(pr))
        yield


def _odd_rows(x, sh):
    k = 1 << sh
    return jnp.concatenate([x[r0:r0 + k] for r0 in range(k, CHUNK, 2 * k)], axis=0)


def _spread_odd_rows(xc, sh):
    k = 1 << sh
    zero = jnp.zeros((k, xc.shape[1]), xc.dtype)
    pieces = []
    for i in range(CHUNK // (2 * k)):
        pieces += [zero, xc[i * k:(i + 1) * k]]
    return jnp.concatenate(pieces, axis=0)


def _zero_row(x):
    row = x[0:1, :]
    return jnp.where(row != row, 1.0, 0.0)


def _solver(g_ref, gnw_ref, gnb_ref, ones_pair, out_ref, st_ref, scrs, done, clock):
    row = lax.broadcasted_iota(jnp.int32, (CHUNK, CHUNK), 0)
    col = lax.broadcasted_iota(jnp.int32, (CHUNK, CHUNK), 1)
    strict = col < row
    incl2 = jnp.concatenate([col <= row, col <= row], axis=1)
    lo = col < HEAD_DIM
    owns = [lo, jnp.logical_not(lo)]
    same_head = (row >> 6) == (col >> 6)
    eye = row == col
    blk8 = (row >> 3) == (col >> 3)
    n_ck = len(scrs)
    heads = [(ck, pr, hh) for ck in range(n_ck) for pr in range(PAIRS) for hh in range(2)]
    sls = [slice(pr * LANES, (pr + 1) * LANES) for pr in range(PAIRS)]
    lhs_refs, rhs_refs, ht_refs, vsw_refs, vso_refs, as_refs, rs_refs, bonus_refs, gam_refs = zip(*scrs)

    taken = [0]

    def tie(x):
        rows = [r for t, r in done[taken[0]:] if t <= clock[0] - _DONE_LAG]
        taken[0] += len(rows)
        return x + sum(rows) if rows else x

    zrows = jnp.zeros((HEAD_DIM, 2 * CHUNK), BF16)

    def head_rows(ref, pr, hh):
        rows = ref[pr * LANES + hh * HEAD_DIM:pr * LANES + (hh + 1) * HEAD_DIM, :]
        return jnp.concatenate([rows, zrows] if hh == 0 else [zrows, rows], axis=0)

    amats = [jnp.dot(lhs_refs[ck][:, sls[pr]], head_rows(rhs_refs[ck], pr, hh), preferred_element_type=F32)
             for ck, pr, hh in heads]
    yield
    n_mats = [jnp.where(strict, am[:CHUNK, :CHUNK], 0.0) for am in amats]
    avs, kvs = [], []
    for ck in range(n_ck):
        for pr in range(PAIRS):
            i0 = (ck * PAIRS + pr) * 2
            stacked = jnp.concatenate(
                [jnp.where(strict, amats[i0 + hh][:CHUNK, CHUNK:], 0.0).astype(BF16) for hh in range(2)]
                + [ht_refs[ck][RWKV_WIDTH + pr * LANES:RWKV_WIDTH + (pr + 1) * LANES, :]], axis=0)
            out = jnp.dot(stacked, vsw_refs[ck][:, sls[pr]], preferred_element_type=F32)
            avs += [out[:CHUNK], out[CHUNK:2 * CHUNK]]
            kvs.append(out[2 * CHUNK:])
    nds = [jnp.where(blk8, n, 0.0) for n in n_mats]
    ndbs = [nd.astype(BF16) for nd in nds]
    s2s = [jnp.dot(ndb, ndb, preferred_element_type=F32) for ndb in ndbs]
    yield
    ps = [jnp.where(eye, 1.0, nd) for nd in nds]
    pss = [_mm(jnp.concatenate([p, s2], axis=0), s2) for p, s2 in zip(ps, s2s)]
    yield
    ps = [p + x[:CHUNK] for p, x in zip(ps, pss)]
    d8s = [p + _mm(p, x[CHUNK:]) for p, x in zip(ps, pss)]
    dbs = [d.astype(BF16) for d in d8s]
    yield
    for sh in (3, 4, 5):
        off = ((row >> (sh + 1)) == (col >> (sh + 1))) & ((row >> sh) != (col >> sh))
        offs = [_odd_rows(jnp.where(off, n, 0.0), sh) for n in n_mats]
        offs[0] = tie(offs[0])
        ts = [jnp.dot(o.astype(BF16), db, preferred_element_type=F32) for o, db in zip(offs, dbs)]
        yield
        d_odd = [_odd_rows(d8, sh).astype(BF16) for d8 in d8s] if sh == 3 else [_odd_rows(db, sh) for db in dbs]
        ts[0] = tie(ts[0])
        us = [jnp.dot(do, _spread_odd_rows(t, sh).astype(BF16), preferred_element_type=F32)
              for do, t in zip(d_odd, ts)]
        dbs = [db + _spread_odd_rows(u, sh).astype(BF16) for db, u in zip(dbs, us)]
        yield
    half = CHUNK // 2
    x0s = [jnp.where(owns[hh], as_refs[ck][:, sls[pr]], av) for (ck, pr, hh), av in zip(heads, avs)]
    x0s[0] = tie(x0s[0])
    y0s = [jnp.dot(db, x0.astype(BF16), preferred_element_type=F32) for db, x0 in zip(dbs, x0s)]
    yield
    y0bs = [y0.astype(BF16) for y0 in y0s]
    lo_half = lax.broadcasted_iota(jnp.int32, (half, CHUNK), 1) < half
    n21s = [jnp.where(lo_half, n[half:], 0.0) for n in n_mats]
    n21s[0] = tie(n21s[0])
    ws = [jnp.dot(n21.astype(BF16), y0b, preferred_element_type=F32)
          for n21, y0b in zip(n21s, y0bs)]
    yield
    ws[0] = tie(ws[0])
    xss = [jnp.concatenate(
        [y0b[:half],
         (y0[half:] + jnp.dot(db[half:], jnp.concatenate([y0b[:half], w.astype(BF16)], axis=0),
                              preferred_element_type=F32)).astype(BF16)], axis=0)
        for y0, y0b, db, w in zip(y0s, y0bs, dbs, ws)]
    yield
    zpad = jnp.zeros((LANES, CHUNK), BF16)
    zrs = [jnp.dot(jnp.concatenate([jnp.where(incl2, am[CHUNK:, :], 0.0).astype(BF16),
                                    jnp.concatenate([ht_refs[ck][sls[pr], :], zpad], axis=1)], axis=0),
                   jnp.concatenate([xs, vso_refs[ck][hh, :, sls[pr]]], axis=0), preferred_element_type=F32)
           for (ck, pr, hh), am, xs in zip(heads, amats, xss)]
    zs = [zr[:CHUNK] for zr in zrs]
    res = [zr[CHUNK:] for zr in zrs]
    yield
    ys = [[] for _ in range(n_ck)]
    for pr in range(PAIRS):
        sl = sls[pr]
        st = st_ref[pr]
        for ck in range(n_ck):
            i0 = (ck * PAIRS + pr) * 2
            z0, z1 = zs[i0], zs[i0 + 1]
            cat = jnp.concatenate([res[i0][:HEAD_DIM], res[i0 + 1][HEAD_DIM:]], axis=0)
            m_t = jnp.where(same_head, cat, 0.0) + jnp.where(eye, gam_refs[ck][0:1, sl], 0.0)
            g_t = jnp.where(same_head, 0.0, cat + kvs[ck * PAIRS + pr])
            q_hat = rs_refs[ck][:, sl] + jnp.where(lo, z0, z1)
            if pr == 0 and ck == 0:
                q_hat = tie(q_hat)
            y_loc_sw = jnp.where(lo, z1, z0)
            qm = _mm(jnp.concatenate([q_hat, m_t], axis=0), st)
            ys[ck].append(qm[:CHUNK] + y_loc_sw)
            st = qm[CHUNK:] + g_t
        st_ref[pr] = st
    yield
    y_all = jnp.concatenate([pltpu.roll(y_sw, HEAD_DIM, 1) for ck in range(n_ck) for y_sw in ys[ck]], axis=0)
    mean = _mm(y_all, ones_pair) * (1.0 / HEAD_DIM)
    dlt = y_all - mean
    var = _mm(dlt * dlt, ones_pair) * (1.0 / HEAD_DIM)
    yn_all = dlt * lax.rsqrt(var + GN_EPS)
    for ck in range(n_ck):
        rows = slice(ck * CHUNK, (ck + 1) * CHUNK)
        for pr in range(PAIRS):
            sl = sls[pr]
            blk = slice((ck * PAIRS + pr) * CHUNK, (ck * PAIRS + pr + 1) * CHUNK)
            g = g_ref[0, rows, sl]
            yn = yn_all[blk] * gnw_ref[:, sl] + gnb_ref[:, sl]
            out_ref[0, rows, sl] = ((yn + bonus_refs[ck][:, sl]) * (g * _sigmoid(g))).astype(out_ref.dtype)
    yield


_FE_SLOTS = ("", "", "", "", "", "", "d0", "p0", "", "d1", "p1", "", "d2", "p2", "", "d3", "p3")
_DONE_LAG = 1


def _rwkv_kernel(ps_ref, prev_ref, g_ref, mu_ref, w0_ref, wup_ref, a0_ref, aup_ref,
                 kkw_ref, ka_ref, rk_ref, gnw_ref, gnb_ref, ones_ref, tri_ref,
                 out_ref, st_ref, *chunk_scr, last):
    s = pl.program_id(1)
    sets = [chunk_scr[i * N_CHUNK_SCRATCH:(i + 1) * N_CHUNK_SCRATCH] for i in range(2 * CHUNKS_PER_STEP)]
    slot_a, slot_b = sets[:CHUNKS_PER_STEP], sets[CHUNKS_PER_STEP:]
    params = (mu_ref, w0_ref, a0_ref, kkw_ref, ka_ref, rk_ref)

    def step(fe_scrs, sv_scrs):
        done, clock = [], [0]
        svs = []
        if sv_scrs is not None:
            svs.append(_solver(g_ref, gnw_ref, gnb_ref, ones_ref[...], out_ref, st_ref, sv_scrs, done, clock))
        fes = []
        if fe_scrs is not None:
            prev_lasts = [prev_ref[0, 7:8, :] * jnp.where(s > 0, 1.0, 0.0)]
            prev_lasts += [ps_ref[0, ck * CHUNK - 1:ck * CHUNK, :] for ck in range(1, CHUNKS_PER_STEP)]
            wa = jnp.concatenate(
                [_token_shift(ps_ref, slice(ck * CHUNK, (ck + 1) * CHUNK), slice(3 * RWKV_WIDTH, SHIFT_WIDTH),
                              prev_lasts[ck], mu_ref) for ck in range(CHUNKS_PER_STEP)], axis=0)
            wa_b = wa.astype(BF16)
            wt_b = jnp.tanh(wa).astype(BF16)
            lw = jnp.dot(wt_b, wup_ref[...], preferred_element_type=F32)
            la = jnp.dot(wa_b, aup_ref[...], preferred_element_type=F32)
            for ck in range(CHUNKS_PER_STEP):
                rows = slice(ck * CHUNK, (ck + 1) * CHUNK)
                fes.append(_frontend(ps_ref, ck, prev_lasts[ck], lw[rows], la[rows], params, ones_ref[...],
                                     tri_ref[...], fe_scrs[ck], done, clock))
        for _ in range(len(_FE_SLOTS)):
            for gen in fes + svs:
                next(gen, None)
            clock[0] += 1
        for gen in svs:
            for _ in gen:
                clock[0] += 1
        for gen in fes:
            for _ in gen:
                pass

    @pl.when(s == 0)
    def _():
        st_ref[...] = jnp.zeros_like(st_ref)
        step(slot_a, None)

    slots = [slot_a, slot_b]
    for parity in range(2):
        @pl.when((s > 0) & (s < last) & ((s & 1) == parity))
        def _():
            step(slots[parity], slots[1 - parity])

    @pl.when(s == last)
    def _():
        step(None, slots[1 - last % 2])


def _rwkv(ps, g_r, mu, w0, wup_pad, a0, aup_pad, kkw, ka, rk, gnw, gnb, ones_pair, tri):
    bsz, seq, _ = ps.shape
    ns = seq // STEP_ROWS
    sub = STEP_ROWS // 8

    def vec(width):
        return pl.BlockSpec((1, width), lambda b, s: (0, 0))

    def fe_block(s):
        return jnp.minimum(s, ns - 1)

    def sv_block(s):
        return jnp.maximum(s - 1, 0)

    return pl.pallas_call(
        functools.partial(_rwkv_kernel, last=ns),
        grid=(bsz, ns + 1),
        in_specs=[
            pl.BlockSpec((1, STEP_ROWS, SHIFT_WIDTH), lambda b, s: (b, fe_block(s), 0)),
            pl.BlockSpec((1, 8, SHIFT_WIDTH), lambda b, s: (b, jnp.maximum(fe_block(s) * sub - 1, 0), 0)),
            pl.BlockSpec((1, STEP_ROWS, RWKV_WIDTH), lambda b, s: (b, sv_block(s), 0)),
            vec(SHIFT_WIDTH),
            vec(RWKV_WIDTH),
            pl.BlockSpec((LANES, RWKV_WIDTH), lambda b, s: (0, 0)),
            vec(RWKV_WIDTH),
            pl.BlockSpec((LANES, RWKV_WIDTH), lambda b, s: (0, 0)),
            vec(RWKV_WIDTH), vec(RWKV_WIDTH), vec(RWKV_WIDTH), vec(RWKV_WIDTH), vec(RWKV_WIDTH),
            pl.BlockSpec((LANES, LANES), lambda b, s: (0, 0)),
            pl.BlockSpec((CHUNK, CHUNK), lambda b, s: (0, 0)),
        ],
        out_specs=pl.BlockSpec((1, STEP_ROWS, RWKV_WIDTH), lambda b, s: (b, sv_block(s), 0)),
        out_shape=jax.ShapeDtypeStruct((bsz, seq, RWKV_WIDTH), BF16),
        scratch_shapes=[pltpu.VMEM((PAIRS, LANES, LANES), F32)]
        + [buf for _ in range(2 * CHUNKS_PER_STEP) for buf in _chunk_scratch()],
        compiler_params=pltpu.CompilerParams(
            dimension_semantics=("arbitrary", "arbitrary"), vmem_limit_bytes=VMEM_LIMIT),
        name="rwkv7_chunked",
    )(ps, ps, g_r, mu, w0, wup_pad, a0, aup_pad, kkw, ka, rk, gnw, gnb, ones_pair, tri)


def _attn_kernel(sinks_ref, q_ref, kvc_ref, kvp_ref, g_ref, o_ref):
    n = pl.program_id(1)
    kv = jnp.concatenate([kvp_ref[0], kvc_ref[0]], axis=0)
    k_nat = kv[:, :KV_WIDTH]
    v_nat = kv[:, KV_WIDTH:]
    ks = [k_nat.astype(BF16), pltpu.roll(k_nat, HEAD_DIM, 1).astype(BF16)]
    vts = [v_nat.T.astype(BF16), pltpu.roll(v_nat, HEAD_DIM, 1).T.astype(BF16)]
    ones_rows = jnp.ones((HEAD_DIM, vts[0].shape[1]), BF16)
    vts = [[jnp.concatenate([vt[:HEAD_DIM], ones_rows], axis=0), jnp.concatenate([ones_rows, vt[HEAD_DIM:]], axis=0)]
           for vt in vts]
    si = lax.broadcasted_iota(jnp.int32, (2 * WINDOW, WINDOW), 0)
    qi = lax.broadcasted_iota(jnp.int32, (2 * WINDOW, WINDOW), 1)
    band = (si > qi) & (si <= qi + WINDOW)
    first_valid = jnp.where(n > 0, 0, WINDOW)
    lo = lax.broadcasted_iota(jnp.int32, (WINDOW, LANES), 1) < HEAD_DIM
    owns = [lo, jnp.logical_not(lo)]
    qs = q_ref[0] * (HEAD_DIM ** -0.5 * LOG2E)
    n_blk = ATTN_WIDTH // LANES
    group = n_blk // 2

    def scores(j):
        rows = slice(j * WINDOW, (j + 1) * WINDOW)
        keys = slice(j * WINDOW, (j + 2) * WINDOW)
        out = []
        for blk in range(n_blk):
            qb = qs[rows, blk * LANES:(blk + 1) * LANES]
            for hf in range(2):
                kk = ks[0 if blk // group == hf else 1][keys]
                out.append(lax.dot_general(kk, jnp.where(owns[hf], qb, 0.0).astype(BF16),
                                           (((1,), (1,)), ((), ())), preferred_element_type=F32))
        return out

    def finish(j, sc):
        rows = slice(j * WINDOW, (j + 1) * WINDOW)
        keys = slice(j * WINDOW, (j + 2) * WINDOW)
        mask = band & (si >= first_valid) if j == 0 else band
        outs = []
        for blk in range(n_blk):
            halves = []
            for hf in range(2):
                s = jnp.where(mask, sc[2 * blk + hf], -1e30)
                sink = sinks_ref[2 * blk + hf] * LOG2E
                m = jnp.maximum(jnp.max(s, axis=0, keepdims=True), sink)
                p = jnp.exp2(s - m).astype(BF16)
                vt = vts[0 if blk // group == hf else 1][hf][:, keys]
                pv = jnp.dot(vt, p, preferred_element_type=F32)
                num, psum = (pv[:HEAD_DIM], pv[HEAD_DIM:HEAD_DIM + 1]) if hf == 0 else (pv[HEAD_DIM:], pv[0:1])
                halves.append(num / (psum + jnp.exp2(sink - m)))
            outs.append(jnp.concatenate(halves, axis=0).T)
        g = g_ref[0, rows, :]
        o_ref[0, rows, :] = (jnp.concatenate(outs, axis=1) * (g * _sigmoid(g))).astype(o_ref.dtype)

    sc = scores(0)
    for j in range(ATTN_Q_BLOCKS):
        nxt = scores(j + 1) if j + 1 < ATTN_Q_BLOCKS else None
        finish(j, sc)
        sc = nxt


def _attn(sinks, q, kv, g_a):
    bsz, seq, _ = q.shape
    nb = seq // ATTN_ROWS
    return pl.pallas_call(
        _attn_kernel,
        grid=(bsz, nb),
        in_specs=[
            pl.BlockSpec(memory_space=pltpu.SMEM),
            pl.BlockSpec((1, ATTN_ROWS, ATTN_WIDTH), lambda b, n: (b, n, 0)),
            pl.BlockSpec((1, ATTN_ROWS, 2 * KV_WIDTH), lambda b, n: (b, n, 0)),
            pl.BlockSpec((1, WINDOW, 2 * KV_WIDTH), lambda b, n: (b, jnp.maximum(n * ATTN_Q_BLOCKS - 1, 0), 0)),
            pl.BlockSpec((1, ATTN_ROWS, ATTN_WIDTH), lambda b, n: (b, n, 0)),
        ],
        out_specs=pl.BlockSpec((1, ATTN_ROWS, ATTN_WIDTH), lambda b, n: (b, n, 0)),
        out_shape=jax.ShapeDtypeStruct((bsz, seq, ATTN_WIDTH), BF16),
        compiler_params=pltpu.CompilerParams(
            dimension_semantics=("arbitrary", "arbitrary"), vmem_limit_bytes=VMEM_LIMIT),
        name="swa_sink",
    )(sinks, q, kv, kv, g_a)


def _final_kernel(ya_ref, yb_ref, x_ref, w_ref, fg_ref, o_ref):
    for rows in _sub_rows(x_ref.shape[0]):
        xn = _out_residual(ya_ref, yb_ref, x_ref, w_ref, rows)
        ms = jnp.mean(xn * xn, axis=-1, keepdims=True)
        o_ref[rows, :] = xn * lax.rsqrt(ms + NORM_EPS) * fg_ref[...]


def _final(ya, yb, x2, w_bf16, fgain):
    rows = x2.shape[0]
    tm = ROW_TILE_FINAL
    row_spec = lambda w: pl.BlockSpec((tm, w), lambda i: (i, 0))
    return pl.pallas_call(
        _final_kernel,
        grid=(rows // tm,),
        in_specs=[row_spec(RWKV_WIDTH), row_spec(ATTN_WIDTH), row_spec(D_MODEL),
                  _resident((D_MODEL, D_MODEL)), _resident((1, D_MODEL))],
        out_specs=row_spec(D_MODEL),
        out_shape=jax.ShapeDtypeStruct((rows, D_MODEL), F32),
        compiler_params=pltpu.CompilerParams(
            dimension_semantics=("arbitrary",), vmem_limit_bytes=VMEM_LIMIT),
        name="outproj_final",
    )(ya, yb, x2, w_bf16, fgain)


def kernel(x, norm_gain, w_in, shift_mu, w0, w_up, a0, a_up, k_k, k_a, r_k, gn_w, gn_b, sinks, w_out, final_gain):
    bsz, seq, d_model = x.shape
    depth = w_in.shape[0]
    rows = bsz * seq
    assert d_model == D_MODEL and w_in.shape[1:] == (D_MODEL, IN_WIDTH) and w_out.shape[1:] == (D_MODEL, D_MODEL)
    assert seq % ATTN_ROWS == 0 and seq % STEP_ROWS == 0 and seq % ROW_TILE == 0 and rows % ROW_TILE_FINAL == 0

    head_id = np.arange(LANES) // HEAD_DIM
    ones_pair = jnp.asarray(head_id[:, None] == head_id[None, :], dtype=BF16)
    tri = jnp.asarray(np.tril(np.ones((CHUNK, CHUNK), np.float32)), dtype=BF16)
    zpad = jnp.zeros((LORA, RWKV_WIDTH), F32)
    w_in_b = w_in.astype(BF16)
    w_out_b = w_out.astype(BF16)

    x2 = x.reshape(rows, D_MODEL)
    proj = _inproj(x2, norm_gain[0][None, :], w_in_b[0])
    for l in range(depth):
        ps, g_r, q, kv, g_a = proj
        wup_pad = jnp.concatenate([w_up[l], zpad], axis=0).astype(BF16)
        aup_pad = jnp.concatenate([zpad, a_up[l]], axis=0).astype(BF16)
        y_a = _rwkv(ps.reshape(bsz, seq, SHIFT_WIDTH), g_r.reshape(bsz, seq, RWKV_WIDTH),
                    shift_mu[l][None, :], w0[l][None, :], wup_pad, a0[l][None, :], aup_pad,
                    k_k[l][None, :], k_a[l][None, :], r_k[l].reshape(1, RWKV_WIDTH),
                    gn_w[l][None, :], gn_b[l][None, :], ones_pair, tri)
        y_b = _attn(sinks[l], q.reshape(bsz, seq, ATTN_WIDTH), kv.reshape(bsz, seq, 2 * KV_WIDTH),
                    g_a.reshape(bsz, seq, ATTN_WIDTH))
        y_a = y_a.reshape(rows, RWKV_WIDTH)
        y_b = y_b.reshape(rows, ATTN_WIDTH)
        if l + 1 < depth:
            x2, *proj = _mid(y_a, y_b, x2, w_out_b[l], norm_gain[l + 1][None, :], w_in_b[l + 1])
        else:
            x2 = _final(y_a, y_b, x2, w_out_b[l], final_gain[None, :])
    return x2.reshape(bsz, seq, D_MODEL)
```

```python
import functools

import numpy as np
import jax
import jax.numpy as jnp
from jax import lax
from jax.experimental import pallas as pl
from jax.experimental.pallas import tpu as pltpu

D_MODEL = 1024
HEAD_DIM = 64
RWKV_WIDTH = 512
ATTN_WIDTH = 512
KV_WIDTH = 128
LORA = 64
SHIFT_WIDTH = 3 * RWKV_WIDTH + 2 * LORA
IN_WIDTH = SHIFT_WIDTH + RWKV_WIDTH + ATTN_WIDTH + 2 * KV_WIDTH + ATTN_WIDTH
NORM_EPS = 1e-5
GN_EPS = 64e-5
WINDOW = 128
CHUNK = 128
LANES = 128
PAIRS = RWKV_WIDTH // LANES
CHUNKS_PER_STEP = 2
STEP_ROWS = CHUNKS_PER_STEP * CHUNK
ATTN_Q_BLOCKS = 8
ATTN_ROWS = ATTN_Q_BLOCKS * WINDOW
LOG2E = 1.4426950408889634
ROW_TILE = 512
ROW_TILE_FINAL = 1024
SUB_ROWS = 256
VMEM_LIMIT = 48 * 1024 * 1024

F32 = jnp.float32
BF16 = jnp.bfloat16


def _mm(a, b):
    return jnp.dot(a.astype(BF16), b.astype(BF16), preferred_element_type=F32)


def _sigmoid(x):
    return 0.5 * jnp.tanh(0.5 * x) + 0.5


PROJ_WIDTHS = (SHIFT_WIDTH, RWKV_WIDTH, ATTN_WIDTH, 2 * KV_WIDTH, ATTN_WIDTH)
PROJ_DTYPES = (F32, F32, BF16, BF16, F32)
PROJ_SCALES = (None, None, HEAD_DIM ** -0.5 * LOG2E, None, None)


def _sub_rows(tm):
    return [slice(r0, r0 + SUB_ROWS) for r0 in range(0, tm, SUB_ROWS)]


def _norm_project(xs, gain_ref, w_ref, out_refs):
    hs = []
    for x in xs:
        ms = jnp.mean(x * x, axis=-1, keepdims=True)
        hs.append((x * lax.rsqrt(ms + NORM_EPS) * gain_ref[...]).astype(BF16))
    for h, rows in zip(hs, _sub_rows(out_refs[0].shape[0])):
        c0 = 0
        for ref, scale in zip(out_refs, PROJ_SCALES):
            w = ref.shape[-1]
            y = jnp.dot(h, w_ref[:, c0:c0 + w], preferred_element_type=F32)
            ref[rows, :] = (y if scale is None else y * scale).astype(ref.dtype)
            c0 += w


def _inproj_kernel(x_ref, gain_ref, w_ref, *out_refs):
    _norm_project([x_ref[rows, :] for rows in _sub_rows(x_ref.shape[0])], gain_ref, w_ref, out_refs)


def _resident(shape):
    return pl.BlockSpec(shape, lambda i: (0,) * len(shape), pipeline_mode=pl.Buffered(1))


def _inproj(x2, gain, w_bf16):
    rows = x2.shape[0]
    tm = ROW_TILE
    return pl.pallas_call(
        _inproj_kernel,
        grid=(rows // tm,),
        in_specs=[
            pl.BlockSpec((tm, D_MODEL), lambda i: (i, 0)),
            _resident((1, D_MODEL)),
            _resident((D_MODEL, IN_WIDTH)),
        ],
        out_specs=[pl.BlockSpec((tm, w), lambda i: (i, 0)) for w in PROJ_WIDTHS],
        out_shape=[jax.ShapeDtypeStruct((rows, w), dt) for w, dt in zip(PROJ_WIDTHS, PROJ_DTYPES)],
        compiler_params=pltpu.CompilerParams(
            dimension_semantics=("arbitrary",), vmem_limit_bytes=VMEM_LIMIT),
        name="inproj",
    )(x2, gain, w_bf16)


def _out_residual(ya_ref, yb_ref, x_ref, w_ref, rows):
    y = (jnp.dot(ya_ref[rows, :], w_ref[:RWKV_WIDTH, :], preferred_element_type=F32)
         + jnp.dot(yb_ref[rows, :], w_ref[RWKV_WIDTH:, :], preferred_element_type=F32))
    return x_ref[rows, :] + y


def _mid_kernel(ya_ref, yb_ref, x_ref, wo_ref, gain_ref, wi_ref, xo_ref, *out_refs):
    xs = []
    for rows in _sub_rows(x_ref.shape[0]):
        xn = _out_residual(ya_ref, yb_ref, x_ref, wo_ref, rows)
        xo_ref[rows, :] = xn
        xs.append(xn)
    _norm_project(xs, gain_ref, wi_ref, out_refs)


def _mid(ya, yb, x2, wo_bf16, gain, wi_bf16):
    rows = x2.shape[0]
    tm = ROW_TILE
    row_spec = lambda w: pl.BlockSpec((tm, w), lambda i: (i, 0))
    return pl.pallas_call(
        _mid_kernel,
        grid=(rows // tm,),
        in_specs=[row_spec(RWKV_WIDTH), row_spec(ATTN_WIDTH), row_spec(D_MODEL),
                  _resident((D_MODEL, D_MODEL)), _resident((1, D_MODEL)), _resident((D_MODEL, IN_WIDTH))],
        out_specs=[row_spec(D_MODEL)] + [row_spec(w) for w in PROJ_WIDTHS],
        out_shape=[jax.ShapeDtypeStruct((rows, D_MODEL), F32)]
        + [jax.ShapeDtypeStruct((rows, w), dt) for w, dt in zip(PROJ_WIDTHS, PROJ_DTYPES)],
        compiler_params=pltpu.CompilerParams(
            dimension_semantics=("arbitrary",), vmem_limit_bytes=VMEM_LIMIT),
        name="outproj_inproj",
    )(ya, yb, x2, wo_bf16, gain, wi_bf16)


def _chunk_scratch():
    return [
        pltpu.VMEM((2 * CHUNK, RWKV_WIDTH), BF16),
        pltpu.VMEM((RWKV_WIDTH, 2 * CHUNK), BF16),
        pltpu.VMEM((2 * RWKV_WIDTH, CHUNK), BF16),
        pltpu.VMEM((CHUNK, RWKV_WIDTH), BF16),
        pltpu.VMEM((2, CHUNK, RWKV_WIDTH), BF16),
        pltpu.VMEM((CHUNK, RWKV_WIDTH), F32),
        pltpu.VMEM((CHUNK, RWKV_WIDTH), F32),
        pltpu.VMEM((CHUNK, RWKV_WIDTH), F32),
        pltpu.VMEM((8, RWKV_WIDTH), F32),
    ]


N_CHUNK_SCRATCH = len(_chunk_scratch())


def _token_shift(ps_ref, rows, cols, prev_last, mu_ref):
    first_row = lax.broadcasted_iota(jnp.int32, (CHUNK, 1), 0) == 0
    p = ps_ref[0, rows, cols]
    prev = jnp.where(first_row, prev_last[:, cols], pltpu.roll(p, 1, 0))
    return p + (prev - p) * mu_ref[:, cols]


def _frontend(ps_ref, ck, prev_last, lw, la, params, ones_pair, tri, scr, done, clock):
    mu_ref, w0_ref, a0_ref, kkw_ref, ka_ref, rk_ref = params
    lhs_ref, rhs_ref, ht_ref, vsw_ref, vso_ref, as_ref, rs_ref, bonus_ref, gam_ref = scr
    rows = slice(ck * CHUNK, (ck + 1) * CHUNK)
    lo = lax.broadcasted_iota(jnp.int32, (CHUNK, LANES), 1) < HEAD_DIM

    def shifted(cols):
        return _token_shift(ps_ref, rows, cols, prev_last, mu_ref)

    pre = []
    for pr in range(PAIRS):
        sl = slice(pr * LANES, (pr + 1) * LANES)
        r = shifted(slice(pr * LANES, (pr + 1) * LANES))
        k = shifted(slice(RWKV_WIDTH + pr * LANES, RWKV_WIDTH + (pr + 1) * LANES))
        v = shifted(slice(2 * RWKV_WIDTH + pr * LANES, 2 * RWKV_WIDTH + (pr + 1) * LANES))
        kkraw = k * kkw_ref[:, sl]
        sq_b = (kkraw * kkraw).astype(BF16)
        ld = _sigmoid(w0_ref[:, sl] + lw[:, sl]) * float(-LOG2E * np.exp(-0.5))
        a = _sigmoid(a0_ref[:, sl] + la[:, sl])
        kp = k * (1.0 + (a - 1.0) * ka_ref[:, sl])
        rkp_b = (r * kp * rk_ref[:, sl]).astype(BF16)
        l_hi = ld.astype(BF16)
        l_lo = (ld - l_hi.astype(F32)).astype(BF16)
        pre.append((r, v, kkraw, ld, a, kp, sq_b, rkp_b, l_hi, l_lo))

    def issue(pr):
        sq_b, rkp_b, l_hi, l_lo = pre[pr][6:]
        sums = jnp.dot(jnp.concatenate([sq_b, rkp_b], axis=0), ones_pair, preferred_element_type=F32)
        cs = jnp.dot(tri, l_hi, preferred_element_type=F32) + jnp.dot(tri, l_lo, preferred_element_type=F32)
        return sums, cs

    def finish(pr, sums, cs):
        sl = slice(pr * LANES, (pr + 1) * LANES)
        r, v, kkraw, ld, a, kp = pre[pr][:6]
        bonus_ref[:, sl] = sums[CHUNK:] * v
        kk = kkraw * lax.rsqrt(jnp.maximum(sums[:CHUNK], 1e-24))
        bv = kk * a
        mid = cs[CHUNK // 2 - 1:CHUNK // 2, :]
        end = cs[CHUNK - 1:CHUNK, :]
        e_out = jnp.exp2(mid - cs)
        e_end = jnp.exp2(end - cs)
        a_t = -kk * jnp.exp2(cs - ld - mid)
        r_t = r * jnp.exp2(cs - mid)
        g_mid = jnp.exp2(mid)
        a_s = a_t * g_mid
        r_s = r_t * g_mid
        as_ref[:, sl] = a_s
        rs_ref[:, sl] = r_s
        gam_ref[:, sl] = jnp.broadcast_to(jnp.exp2(end), (gam_ref.shape[0], LANES))
        lhs_ref[:CHUNK, sl] = a_t.astype(BF16)
        lhs_ref[CHUNK:, sl] = r_t.astype(BF16)
        b_out, k_out, b_end, k_end = (bv * e_out).T, (kp * e_out).T, (bv * e_end).T, (kp * e_end).T
        rhs_ref[sl, :CHUNK] = b_out.astype(BF16)
        rhs_ref[sl, CHUNK:] = k_out.astype(BF16)
        ht_ref[sl, :] = b_end.astype(BF16)
        ht_ref[RWKV_WIDTH + pr * LANES:RWKV_WIDTH + (pr + 1) * LANES, :] = k_end.astype(BF16)
        v_sw = pltpu.roll(v, HEAD_DIM, 1)
        vsw_ref[:, sl] = v_sw.astype(BF16)
        vso_ref[0, :, sl] = jnp.where(lo, 0.0, v_sw).astype(BF16)
        vso_ref[1, :, sl] = jnp.where(lo, v_sw, 0.0).astype(BF16)
        done.append((clock[0], sum(_zero_row(x) for x in (a_s, r_s, b_out, k_out, b_end, k_end, v_sw))))

    issued = {}
    for slot in _FE_SLOTS:
        for item in slot.split():
            pr = int(item[1])
            if item[0] == "d":
                issued[pr] = issue(pr)
            else:
                finish(pr, *issued.pop(pr))
        yield


def _odd_rows(x, sh):
    k = 1 << sh
    return jnp.concatenate([x[r0:r0 + k] for r0 in range(k, CHUNK, 2 * k)], axis=0)


def _spread_odd_rows(xc, sh):
    k = 1 << sh
    zero = jnp.zeros((k, xc.shape[1]), xc.dtype)
    pieces = []
    for i in range(CHUNK // (2 * k)):
        pieces += [zero, xc[i * k:(i + 1) * k]]
    return jnp.concatenate(pieces, axis=0)


def _zero_row(x):
    row = x[0:1, :]
    return jnp.where(row != row, 1.0, 0.0)


def _solver(g_ref, gnw_ref, gnb_ref, ones_pair, out_ref, st_ref, scrs, done, clock):
    row = lax.broadcasted_iota(jnp.int32, (CHUNK, CHUNK), 0)
    col = lax.broadcasted_iota(jnp.int32, (CHUNK, CHUNK), 1)
    strict = col < row
    incl2 = jnp.concatenate([col <= row, col <= row], axis=1)
    lo = col < HEAD_DIM
    owns = [lo, jnp.logical_not(lo)]
    same_head = (row >> 6) == (col >> 6)
    eye = row == col
    blk8 = (row >> 3) == (col >> 3)
    n_ck = len(scrs)
    heads = [(ck, pr, hh) for ck in range(n_ck) for pr in range(PAIRS) for hh in range(2)]
    sls = [slice(pr * LANES, (pr + 1) * LANES) for pr in range(PAIRS)]
    lhs_refs, rhs_refs, ht_refs, vsw_refs, vso_refs, as_refs, rs_refs, bonus_refs, gam_refs = zip(*scrs)

    taken = [0]

    def tie(x):
        rows = [r for t, r in done[taken[0]:] if t <= clock[0] - _DONE_LAG]
        taken[0] += len(rows)
        return x + sum(rows) if rows else x

    zrows = jnp.zeros((HEAD_DIM, 2 * CHUNK), BF16)

    def head_rows(ref, pr, hh):
        rows = ref[pr * LANES + hh * HEAD_DIM:pr * LANES + (hh + 1) * HEAD_DIM, :]
        return jnp.concatenate([rows, zrows] if hh == 0 else [zrows, rows], axis=0)

    amats = [jnp.dot(lhs_refs[ck][:, sls[pr]], head_rows(rhs_refs[ck], pr, hh), preferred_element_type=F32)
             for ck, pr, hh in heads]
    yield
    n_mats = [jnp.where(strict, am[:CHUNK, :CHUNK], 0.0) for am in amats]
    avs, kvs = [], []
    for ck in range(n_ck):
        for pr in range(PAIRS):
            i0 = (ck * PAIRS + pr) * 2
            stacked = jnp.concatenate(
                [jnp.where(strict, amats[i0 + hh][:CHUNK, CHUNK:], 0.0).astype(BF16) for hh in range(2)]
                + [ht_refs[ck][RWKV_WIDTH + pr * LANES:RWKV_WIDTH + (pr + 1) * LANES, :]], axis=0)
            out = jnp.dot(stacked, vsw_refs[ck][:, sls[pr]], preferred_element_type=F32)
            avs += [out[:CHUNK], out[CHUNK:2 * CHUNK]]
            kvs.append(out[2 * CHUNK:])
    nds = [jnp.where(blk8, n, 0.0) for n in n_mats]
    ndbs = [nd.astype(BF16) for nd in nds]
    s2s = [jnp.dot(ndb, ndb, preferred_element_type=F32) for ndb in ndbs]
    yield
    ps = [jnp.where(eye, 1.0, nd) for nd in nds]
    pss = [_mm(jnp.concatenate([p, s2], axis=0), s2) for p, s2 in zip(ps, s2s)]
    yield
    ps = [p + x[:CHUNK] for p, x in zip(ps, pss)]
    d8s = [p + _mm(p, x[CHUNK:]) for p, x in zip(ps, pss)]
    dbs = [d.astype(BF16) for d in d8s]
    yield
    for sh in (3, 4, 5):
        off = ((row >> (sh + 1)) == (col >> (sh + 1))) & ((row >> sh) != (col >> sh))
        offs = [_odd_rows(jnp.where(off, n, 0.0), sh) for n in n_mats]
        offs[0] = tie(offs[0])
        ts = [jnp.dot(o.astype(BF16), db, preferred_element_type=F32) for o, db in zip(offs, dbs)]
        yield
        d_odd = [_odd_rows(d8, sh).astype(BF16) for d8 in d8s] if sh == 3 else [_odd_rows(db, sh) for db in dbs]
        ts[0] = tie(ts[0])
        us = [jnp.dot(do, _spread_odd_rows(t, sh).astype(BF16), preferred_element_type=F32)
              for do, t in zip(d_odd, ts)]
        dbs = [db + _spread_odd_rows(u, sh).astype(BF16) for db, u in zip(dbs, us)]
        yield
    half = CHUNK // 2
    x0s = [jnp.where(owns[hh], as_refs[ck][:, sls[pr]], av) for (ck, pr, hh), av in zip(heads, avs)]
    x0s[0] = tie(x0s[0])
    y0s = [jnp.dot(db, x0.astype(BF16), preferred_element_type=F32) for db, x0 in zip(dbs, x0s)]
    yield
    y0bs = [y0.astype(BF16) for y0 in y0s]
    lo_half = lax.broadcasted_iota(jnp.int32, (half, CHUNK), 1) < half
    n21s = [jnp.where(lo_half, n[half:], 0.0) for n in n_mats]
    n21s[0] = tie(n21s[0])
    ws = [jnp.dot(n21.astype(BF16), y0b, preferred_element_type=F32)
          for n21, y0b in zip(n21s, y0bs)]
    yield
    ws[0] = tie(ws[0])
    xss = [jnp.concatenate(
        [y0b[:half],
         (y0[half:] + jnp.dot(db[half:], jnp.concatenate([y0b[:half], w.astype(BF16)], axis=0),
                              preferred_element_type=F32)).astype(BF16)], axis=0)
        for y0, y0b, db, w in zip(y0s, y0bs, dbs, ws)]
    yield
    zpad = jnp.zeros((LANES, CHUNK), BF16)
    zrs = [jnp.dot(jnp.concatenate([jnp.where(incl2, am[CHUNK:, :], 0.0).astype(BF16),
                                    jnp.concatenate([ht_refs[ck][sls[pr], :], zpad], axis=1)], axis=0),
                   jnp.concatenate([xs, vso_refs[ck][hh, :, sls[pr]]], axis=0), preferred_element_type=F32)
           for (ck, pr, hh), am, xs in zip(heads, amats, xss)]
    zs = [zr[:CHUNK] for zr in zrs]
    res = [zr[CHUNK:] for zr in zrs]
    yield
    ys = [[] for _ in range(n_ck)]
    for pr in range(PAIRS):
        sl = sls[pr]
        st = st_ref[pr]
        for ck in range(n_ck):
            i0 = (ck * PAIRS + pr) * 2
            z0, z1 = zs[i0], zs[i0 + 1]
            cat = jnp.concatenate([res[i0][:HEAD_DIM], res[i0 + 1][HEAD_DIM:]], axis=0)
            m_t = jnp.where(same_head, cat, 0.0) + jnp.where(eye, gam_refs[ck][0:1, sl], 0.0)
            g_t = jnp.where(same_head, 0.0, cat + kvs[ck * PAIRS + pr])
            q_hat = rs_refs[ck][:, sl] + jnp.where(lo, z0, z1)
            if pr == 0 and ck == 0:
                q_hat = tie(q_hat)
            y_loc_sw = jnp.where(lo, z1, z0)
            qm = _mm(jnp.concatenate([q_hat, m_t], axis=0), st)
            ys[ck].append(qm[:CHUNK] + y_loc_sw)
            st = qm[CHUNK:] + g_t
        st_ref[pr] = st
    yield
    y_all = jnp.concatenate([pltpu.roll(y_sw, HEAD_DIM, 1) for ck in range(n_ck) for y_sw in ys[ck]], axis=0)
    mean = _mm(y_all, ones_pair) * (1.0 / HEAD_DIM)
    dlt = y_all - mean
    var = _mm(dlt * dlt, ones_pair) * (1.0 / HEAD_DIM)
    yn_all = dlt * lax.rsqrt(var + GN_EPS)
    for ck in range(n_ck):
        rows = slice(ck * CHUNK, (ck + 1) * CHUNK)
        for pr in range(PAIRS):
            sl = sls[pr]
            blk = slice((ck * PAIRS + pr) * CHUNK, (ck * PAIRS + pr + 1) * CHUNK)
            g = g_ref[0, rows, sl]
            yn = yn_all[blk] * gnw_ref[:, sl] + gnb_ref[:, sl]
            out_ref[0, rows, sl] = ((yn + bonus_refs[ck][:, sl]) * (g * _sigmoid(g))).astype(out_ref.dtype)
    yield


_FE_SLOTS = ("", "", "", "", "", "", "d0", "p0", "", "d1", "p1", "", "d2", "p2", "", "d3", "p3")
_DONE_LAG = 1


def _rwkv_kernel(ps_ref, prev_ref, g_ref, mu_ref, w0_ref, wup_ref, a0_ref, aup_ref,
                 kkw_ref, ka_ref, rk_ref, gnw_ref, gnb_ref, ones_ref, tri_ref,
                 out_ref, st_ref, *chunk_scr, last):
    s = pl.program_id(1)
    sets = [chunk_scr[i * N_CHUNK_SCRATCH:(i + 1) * N_CHUNK_SCRATCH] for i in range(2 * CHUNKS_PER_STEP)]
    slot_a, slot_b = sets[:CHUNKS_PER_STEP], sets[CHUNKS_PER_STEP:]
    params = (mu_ref, w0_ref, a0_ref, kkw_ref, ka_ref, rk_ref)

    def step(fe_scrs, sv_scrs):
        done, clock = [], [0]
        svs = []
        if sv_scrs is not None:
            svs.append(_solver(g_ref, gnw_ref, gnb_ref, ones_ref[...], out_ref, st_ref, sv_scrs, done, clock))
        fes = []
        if fe_scrs is not None:
            prev_lasts = [prev_ref[0, 7:8, :] * jnp.where(s > 0, 1.0, 0.0)]
            prev_lasts += [ps_ref[0, ck * CHUNK - 1:ck * CHUNK, :] for ck in range(1, CHUNKS_PER_STEP)]
            wa = jnp.concatenate(
                [_token_shift(ps_ref, slice(ck * CHUNK, (ck + 1) * CHUNK), slice(3 * RWKV_WIDTH, SHIFT_WIDTH),
                              prev_lasts[ck], mu_ref) for ck in range(CHUNKS_PER_STEP)], axis=0)
            wa_b = wa.astype(BF16)
            wt_b = jnp.tanh(wa).astype(BF16)
            lw = jnp.dot(wt_b, wup_ref[...], preferred_element_type=F32)
            la = jnp.dot(wa_b, aup_ref[...], preferred_element_type=F32)
            for ck in range(CHUNKS_PER_STEP):
                rows = slice(ck * CHUNK, (ck + 1) * CHUNK)
                fes.append(_frontend(ps_ref, ck, prev_lasts[ck], lw[rows], la[rows], params, ones_ref[...],
                                     tri_ref[...], fe_scrs[ck], done, clock))
        for _ in range(len(_FE_SLOTS)):
            for gen in fes + svs:
                next(gen, None)
            clock[0] += 1
        for gen in svs:
            for _ in gen:
                clock[0] += 1
        for gen in fes:
            for _ in gen:
                pass

    @pl.when(s == 0)
    def _():
        st_ref[...] = jnp.zeros_like(st_ref)
        step(slot_a, None)

    slots = [slot_a, slot_b]
    for parity in range(2):
        @pl.when((s > 0) & (s < last) & ((s & 1) == parity))
        def _():
            step(slots[parity], slots[1 - parity])

    @pl.when(s == last)
    def _():
        step(None, slots[1 - last % 2])


def _rwkv(ps, g_r, mu, w0, wup_pad, a0, aup_pad, kkw, ka, rk, gnw, gnb, ones_pair, tri):
    bsz, seq, _ = ps.shape
    ns = seq // STEP_ROWS
    sub = STEP_ROWS // 8

    def vec(width):
        return pl.BlockSpec((1, width), lambda b, s: (0, 0))

    def fe_block(s):
        return jnp.minimum(s, ns - 1)

    def sv_block(s):
        return jnp.maximum(s - 1, 0)

    return pl.pallas_call(
        functools.partial(_rwkv_kernel, last=ns),
        grid=(bsz, ns + 1),
        in_specs=[
            pl.BlockSpec((1, STEP_ROWS, SHIFT_WIDTH), lambda b, s: (b, fe_block(s), 0)),
            pl.BlockSpec((1, 8, SHIFT_WIDTH), lambda b, s: (b, jnp.maximum(fe_block(s) * sub - 1, 0), 0)),
            pl.BlockSpec((1, STEP_ROWS, RWKV_WIDTH), lambda b, s: (b, sv_block(s), 0)),
            vec(SHIFT_WIDTH),
            vec(RWKV_WIDTH),
            pl.BlockSpec((LANES, RWKV_WIDTH), lambda b, s: (0, 0)),
            vec(RWKV_WIDTH),
            pl.BlockSpec((LANES, RWKV_WIDTH), lambda b, s: (0, 0)),
            vec(RWKV_WIDTH), vec(RWKV_WIDTH), vec(RWKV_WIDTH), vec(RWKV_WIDTH), vec(RWKV_WIDTH),
            pl.BlockSpec((LANES, LANES), lambda b, s: (0, 0)),
            pl.BlockSpec((CHUNK, CHUNK), lambda b, s: (0, 0)),
        ],
        out_specs=pl.BlockSpec((1, STEP_ROWS, RWKV_WIDTH), lambda b, s: (b, sv_block(s), 0)),
        out_shape=jax.ShapeDtypeStruct((bsz, seq, RWKV_WIDTH), BF16),
        scratch_shapes=[pltpu.VMEM((PAIRS, LANES, LANES), F32)]
        + [buf for _ in range(2 * CHUNKS_PER_STEP) for buf in _chunk_scratch()],
        compiler_params=pltpu.CompilerParams(
            dimension_semantics=("arbitrary", "arbitrary"), vmem_limit_bytes=VMEM_LIMIT),
        name="rwkv7_chunked",
    )(ps, ps, g_r, mu, w0, wup_pad, a0, aup_pad, kkw, ka, rk, gnw, gnb, ones_pair, tri)


def _attn_kernel(sinks_ref, q_ref, kvc_ref, kvp_ref, g_ref, o_ref):
    n = pl.program_id(1)
    kv = jnp.concatenate([kvp_ref[0], kvc_ref[0]], axis=0).astype(F32)
    k_nat = kv[:, :KV_WIDTH]
    v_nat = kv[:, KV_WIDTH:]
    ks = [k_nat.astype(BF16), pltpu.roll(k_nat, HEAD_DIM, 1).astype(BF16)]
    vts = [v_nat.T.astype(BF16), pltpu.roll(v_nat, HEAD_DIM, 1).T.astype(BF16)]
    ones_rows = jnp.ones((HEAD_DIM, vts[0].shape[1]), BF16)
    vts = [[jnp.concatenate([vt[:HEAD_DIM], ones_rows], axis=0), jnp.concatenate([ones_rows, vt[HEAD_DIM:]], axis=0)]
           for vt in vts]
    si = lax.broadcasted_iota(jnp.int32, (2 * WINDOW, WINDOW), 0)
    qi = lax.broadcasted_iota(jnp.int32, (2 * WINDOW, WINDOW), 1)
    band = (si > qi) & (si <= qi + WINDOW)
    first_valid = jnp.where(n > 0, 0, WINDOW)
    lo = lax.broadcasted_iota(jnp.int32, (WINDOW, LANES), 1) < HEAD_DIM
    owns = [lo, jnp.logical_not(lo)]
    qs = q_ref[0]
    n_blk = ATTN_WIDTH // LANES
    group = n_blk // 2

    def scores(j):
        rows = slice(j * WINDOW, (j + 1) * WINDOW)
        keys = slice(j * WINDOW, (j + 2) * WINDOW)
        out = []
        for blk in range(n_blk):
            qb = qs[rows, blk * LANES:(blk + 1) * LANES]
            for hf in range(2):
                kk = ks[0 if blk // group == hf else 1][keys]
                out.append(lax.dot_general(kk, jnp.where(owns[hf], qb, jnp.zeros_like(qb)),
                                           (((1,), (1,)), ((), ())), preferred_element_type=F32))
        return out

    def finish(j, sc):
        rows = slice(j * WINDOW, (j + 1) * WINDOW)
        keys = slice(j * WINDOW, (j + 2) * WINDOW)
        mask = band & (si >= first_valid) if j == 0 else band
        outs = []
        for blk in range(n_blk):
            halves = []
            for hf in range(2):
                s = jnp.where(mask, sc[2 * blk + hf], -1e30)
                sink = sinks_ref[2 * blk + hf] * LOG2E
                m = jnp.maximum(jnp.max(s, axis=0, keepdims=True), sink)
                p = jnp.exp2(s - m).astype(BF16)
                vt = vts[0 if blk // group == hf else 1][hf][:, keys]
                pv = jnp.dot(vt, p, preferred_element_type=F32)
                num, psum = (pv[:HEAD_DIM], pv[HEAD_DIM:HEAD_DIM + 1]) if hf == 0 else (pv[HEAD_DIM:], pv[0:1])
                halves.append(num / (psum + jnp.exp2(sink - m)))
            outs.append(jnp.concatenate(halves, axis=0).T)
        g = g_ref[0, rows, :]
        o_ref[0, rows, :] = (jnp.concatenate(outs, axis=1) * (g * _sigmoid(g))).astype(o_ref.dtype)

    sc = scores(0)
    for j in range(ATTN_Q_BLOCKS):
        nxt = scores(j + 1) if j + 1 < ATTN_Q_BLOCKS else None
        finish(j, sc)
        sc = nxt


def _attn(sinks, q, kv, g_a):
    bsz, seq, _ = q.shape
    nb = seq // ATTN_ROWS
    return pl.pallas_call(
        _attn_kernel,
        grid=(bsz, nb),
        in_specs=[
            pl.BlockSpec(memory_space=pltpu.SMEM),
            pl.BlockSpec((1, ATTN_ROWS, ATTN_WIDTH), lambda b, n: (b, n, 0)),
            pl.BlockSpec((1, ATTN_ROWS, 2 * KV_WIDTH), lambda b, n: (b, n, 0)),
            pl.BlockSpec((1, WINDOW, 2 * KV_WIDTH), lambda b, n: (b, jnp.maximum(n * ATTN_Q_BLOCKS - 1, 0), 0)),
            pl.BlockSpec((1, ATTN_ROWS, ATTN_WIDTH), lambda b, n: (b, n, 0)),
        ],
        out_specs=pl.BlockSpec((1, ATTN_ROWS, ATTN_WIDTH), lambda b, n: (b, n, 0)),
        out_shape=jax.ShapeDtypeStruct((bsz, seq, ATTN_WIDTH), BF16),
        compiler_params=pltpu.CompilerParams(
            dimension_semantics=("arbitrary", "arbitrary"), vmem_limit_bytes=VMEM_LIMIT),
        name="swa_sink",
    )(sinks, q, kv, kv, g_a)


def _final_kernel(ya_ref, yb_ref, x_ref, w_ref, fg_ref, o_ref):
    for rows in _sub_rows(x_ref.shape[0]):
        xn = _out_residual(ya_ref, yb_ref, x_ref, w_ref, rows)
        ms = jnp.mean(xn * xn, axis=-1, keepdims=True)
        o_ref[rows, :] = xn * lax.rsqrt(ms + NORM_EPS) * fg_ref[...]


def _final(ya, yb, x2, w_bf16, fgain):
    rows = x2.shape[0]
    tm = ROW_TILE_FINAL
    row_spec = lambda w: pl.BlockSpec((tm, w), lambda i: (i, 0))
    return pl.pallas_call(
        _final_kernel,
        grid=(rows // tm,),
        in_specs=[row_spec(RWKV_WIDTH), row_spec(ATTN_WIDTH), row_spec(D_MODEL),
                  _resident((D_MODEL, D_MODEL)), _resident((1, D_MODEL))],
        out_specs=row_spec(D_MODEL),
        out_shape=jax.ShapeDtypeStruct((rows, D_MODEL), F32),
        compiler_params=pltpu.CompilerParams(
            dimension_semantics=("arbitrary",), vmem_limit_bytes=VMEM_LIMIT),
        name="outproj_final",
    )(ya, yb, x2, w_bf16, fgain)


def kernel(x, norm_gain, w_in, shift_mu, w0, w_up, a0, a_up, k_k, k_a, r_k, gn_w, gn_b, sinks, w_out, final_gain):
    bsz, seq, d_model = x.shape
    depth = w_in.shape[0]
    rows = bsz * seq
    assert d_model == D_MODEL and w_in.shape[1:] == (D_MODEL, IN_WIDTH) and w_out.shape[1:] == (D_MODEL, D_MODEL)
    assert seq % ATTN_ROWS == 0 and seq % STEP_ROWS == 0 and seq % ROW_TILE == 0 and rows % ROW_TILE_FINAL == 0

    head_id = np.arange(LANES) // HEAD_DIM
    ones_pair = jnp.asarray(head_id[:, None] == head_id[None, :], dtype=BF16)
    tri = jnp.asarray(np.tril(np.ones((CHUNK, CHUNK), np.float32)), dtype=BF16)
    zpad = jnp.zeros((LORA, RWKV_WIDTH), F32)
    w_in_b = w_in.astype(BF16)
    w_out_b = w_out.astype(BF16)

    x2 = x.reshape(rows, D_MODEL)
    proj = _inproj(x2, norm_gain[0][None, :], w_in_b[0])
    for l in range(depth):
        ps, g_r, q, kv, g_a = proj
        wup_pad = jnp.concatenate([w_up[l], zpad], axis=0).astype(BF16)
        aup_pad = jnp.concatenate([zpad, a_up[l]], axis=0).astype(BF16)
        y_a = _rwkv(ps.reshape(bsz, seq, SHIFT_WIDTH), g_r.reshape(bsz, seq, RWKV_WIDTH),
                    shift_mu[l][None, :], w0[l][None, :], wup_pad, a0[l][None, :], aup_pad,
                    k_k[l][None, :], k_a[l][None, :], r_k[l].reshape(1, RWKV_WIDTH),
                    gn_w[l][None, :], gn_b[l][None, :], ones_pair, tri)
        y_b = _attn(sinks[l], q.reshape(bsz, seq, ATTN_WIDTH), kv.reshape(bsz, seq, 2 * KV_WIDTH),
                    g_a.reshape(bsz, seq, ATTN_WIDTH))
        y_a = y_a.reshape(rows, RWKV_WIDTH)
        y_b = y_b.reshape(rows, ATTN_WIDTH)
        if l + 1 < depth:
            x2, *proj = _mid(y_a, y_b, x2, w_out_b[l], norm_gain[l + 1][None, :], w_in_b[l + 1])
        else:
            x2 = _final(y_a, y_b, x2, w_out_b[l], final_gain[None, :])
    return x2.reshape(bsz, seq, D_MODEL)
```
